```python
import math
import jax, jax.numpy as jnp
from jax import lax
import numpy as np

D_MODEL = 1024
BATCH = 8
SEQ = 4096
DEPTH = 4

N_MIXERS = 3
N_A_LAYERS = (DEPTH + 2) // 3
N_B_LAYERS = (DEPTH + 1) // 3
N_C_LAYERS = DEPTH // 3
EPS = 1e-6

SB_HEAD_DIM = 64
SB_HEADS = D_MODEL // SB_HEAD_DIM
Q_BLOCK = 128

GM_CHUNK = 128
GM_HALF = 2 * D_MODEL
GM_GROUPS = 16
GM_GROUP_DIM = GM_HALF // GM_GROUPS

SSM_INNER = 2 * D_MODEL
SSM_HEAD_DIM = 64
SSM_HEADS = SSM_INNER // SSM_HEAD_DIM
SSM_GROUPS = 8
SSM_HPG = SSM_HEADS // SSM_GROUPS
SSM_STATE = 128
SSM_CONV = 4
SSM_CHUNK = 128
SSM_CONV_DIM = SSM_INNER + 2 * SSM_GROUPS * SSM_STATE
SSM_PROJ = SSM_INNER + SSM_CONV_DIM + SSM_HEADS

FFN_HIDDEN = -(-8 * D_MODEL // (3 * 256)) * 256

kernel_name = "hybrid_sb_gmlp_ssd_trunk"


def rmsnorm(x, g):
    xf = x.astype(jnp.float32)
    y = xf * lax.rsqrt(jnp.mean(xf * xf, axis=-1, keepdims=True) + EPS)
    return (y * g.astype(jnp.float32)).astype(x.dtype)


def stick_breaking_attention(h, w_qkv, q_gain, k_gain, w_o):
    B_, S, _ = h.shape
    qkv = (h @ w_qkv).reshape(B_, S, 3, SB_HEADS, SB_HEAD_DIM)
    q = rmsnorm(qkv[:, :, 0], q_gain).transpose(0, 2, 1, 3)
    k = rmsnorm(qkv[:, :, 1], k_gain).transpose(0, 2, 1, 3)
    v = qkv[:, :, 2].transpose(0, 2, 1, 3)
    scale = 1.0 / math.sqrt(SB_HEAD_DIM)
    nb = S // Q_BLOCK
    q_blocks = q.reshape(B_, SB_HEADS, nb, Q_BLOCK, SB_HEAD_DIM).transpose(2, 0, 1, 3, 4)
    key_pos = jnp.arange(S)

    def one_block(args):
        qb, start = args
        t = start + jnp.arange(Q_BLOCK)
        z = jnp.einsum('bhtd,bhsd->bhts', qb, k).astype(jnp.float32) * scale
        mask = key_pos[None, :] < t[:, None]
        log_beta = jax.nn.log_sigmoid(z)
        log_1m = jnp.where(mask, jax.nn.log_sigmoid(-z), 0.0)
        suffix = lax.cumsum(log_1m, axis=3, reverse=True) - log_1m
        w = jnp.where(mask, jnp.exp(log_beta + suffix), 0.0)
        return jnp.einsum('bhts,bhsd->bhtd', w.astype(v.dtype), v)

    out = lax.map(one_block, (q_blocks, jnp.arange(nb) * Q_BLOCK))
    out = out.transpose(1, 0, 3, 2, 4).reshape(B_, S, SB_HEADS * SB_HEAD_DIM)
    return out @ w_o


def chunked_gmlp(h, w_in, b_in, v_gain, w_s, b_s, w_out):
    B_, S, _ = h.shape
    zz = jax.nn.gelu(h @ w_in + b_in, approximate=False)
    u, v = zz[..., :GM_HALF], zz[..., GM_HALF:]
    v = rmsnorm(v, v_gain)
    nc = S // GM_CHUNK
    v = v.reshape(B_, nc, GM_CHUNK, GM_GROUPS, GM_GROUP_DIM)
    causal = jnp.tril(jnp.ones((GM_CHUNK, GM_CHUNK), dtype=bool))
    w_s_c = jnp.where(causal, w_s, 0.0).astype(v.dtype)
    mixed = jnp.einsum('gts,bnsgc->bntgc', w_s_c, v) + b_s.T[None, None, :, :, None]
    return (u * mixed.reshape(B_, S, GM_HALF)) @ w_out


def ssd_chunked(x, dt, A, Bm, Cm):
    B_, S, _, _ = x.shape
    nc = S // SSM_CHUNK
    L = SSM_CHUNK
    f32 = jnp.float32
    xdt = (x.astype(f32) * dt[..., None]).reshape(B_, nc, L, SSM_GROUPS, SSM_HPG, SSM_HEAD_DIM)
    a = (dt * A).reshape(B_, nc, L, SSM_GROUPS, SSM_HPG)
    Bc = Bm.astype(f32).reshape(B_, nc, L, SSM_GROUPS, SSM_STATE)
    Cc = Cm.astype(f32).reshape(B_, nc, L, SSM_GROUPS, SSM_STATE)
    a_cum = jnp.cumsum(a, axis=2)

    a_t = a_cum.transpose(0, 1, 3, 4, 2)
    seg = a_t[..., :, None] - a_t[..., None, :]
    causal = jnp.tril(jnp.ones((L, L), dtype=bool))
    decay = jnp.exp(jnp.where(causal, seg, -jnp.inf))
    cb = jnp.einsum('bclgn,bcsgn->bcgls', Cc, Bc)
    y_diag = jnp.einsum('bcgrls,bcsgrp->bclgrp', cb[:, :, :, None] * decay, xdt)

    decay_to_end = jnp.exp(a_cum[:, :, -1:] - a_cum)
    states = jnp.einsum('bclgn,bclgrp->bcgrpn', Bc, xdt * decay_to_end[..., None])
    chunk_decay = jnp.exp(a_cum[:, :, -1])

    def step(hstate, inp):
        st, dec = inp
        return hstate * dec[..., None, None] + st, hstate

    init = jnp.zeros((B_, SSM_GROUPS, SSM_HPG, SSM_HEAD_DIM, SSM_STATE), f32)
    _, prev = lax.scan(step, init, (states.transpose(1, 0, 2, 3, 4, 5),
                                    chunk_decay.transpose(1, 0, 2, 3)))
    prev = prev.transpose(1, 0, 2, 3, 4, 5)

    y_off = jnp.einsum('bclgn,bcgrpn->bclgrp', Cc, prev) * jnp.exp(a_cum)[..., None]
    y = (y_diag + y_off).reshape(B_, S, SSM_HEADS, SSM_HEAD_DIM)
    return y.astype(x.dtype)


def mamba2_mixer(h, w_in, conv_w, conv_b, dt_bias, a_log, d_skip, norm_gain, w_out):
    B_, S, _ = h.shape
    zxbcdt = h @ w_in
    z = zxbcdt[..., :SSM_INNER]
    xbc = zxbcdt[..., SSM_INNER:SSM_INNER + SSM_CONV_DIM]
    dt = zxbcdt[..., SSM_INNER + SSM_CONV_DIM:]
    xbc = lax.conv_general_dilated(
        xbc, conv_w[:, None, :].astype(xbc.dtype), window_strides=(1,),
        padding=[(SSM_CONV - 1, 0)], dimension_numbers=('NWC', 'WIO', 'NWC'),
        feature_group_count=SSM_CONV_DIM) + conv_b
    xbc = jax.nn.silu(xbc)
    xs = xbc[..., :SSM_INNER].reshape(B_, S, SSM_HEADS, SSM_HEAD_DIM)
    Bm = xbc[..., SSM_INNER:SSM_INNER + SSM_GROUPS * SSM_STATE].reshape(B_, S, SSM_GROUPS, SSM_STATE)
    Cm = xbc[..., SSM_INNER + SSM_GROUPS * SSM_STATE:].reshape(B_, S, SSM_GROUPS, SSM_STATE)
    dt = jax.nn.softplus(dt.astype(jnp.float32) + dt_bias.astype(jnp.float32))
    A = -jnp.exp(a_log.astype(jnp.float32))
    y = ssd_chunked(xs, dt, A, Bm, Cm)
    y = (y + xs * d_skip[:, None]).reshape(B_, S, SSM_INNER)
    yg = (y * jax.nn.silu(z)).astype(jnp.float32).reshape(B_, S, SSM_GROUPS, SSM_INNER // SSM_GROUPS)
    yg = yg * lax.rsqrt(jnp.mean(yg * yg, axis=-1, keepdims=True) + EPS)
    y = (yg.reshape(B_, S, SSM_INNER) * norm_gain.astype(jnp.float32)).astype(h.dtype)
    return y @ w_out


def swiglu(h, w_gu, w_down):
    gu = h @ w_gu
    return (jax.nn.silu(gu[..., :FFN_HIDDEN]) * gu[..., FFN_HIDDEN:]) @ w_down


def setup_inputs(seed: int = 0) -> dict:
    key = jax.random.key(seed)
    ks = iter(jax.random.split(key, 40))

    def nrm(shape, scale):
        return jax.random.normal(next(ks), shape, jnp.float32) * scale

    def gain(shape):
        return 1.0 + nrm(shape, 0.02)

    D = D_MODEL
    x = nrm((BATCH, SEQ, D), 1.0)
    mix_norm = gain((DEPTH, D))
    ffn_norm = gain((DEPTH, D))
    sb_w_qkv = nrm((N_A_LAYERS, D, 3 * SB_HEADS * SB_HEAD_DIM), D ** -0.5)
    sb_q_gain = gain((N_A_LAYERS, SB_HEAD_DIM))
    sb_k_gain = gain((N_A_LAYERS, SB_HEAD_DIM))
    sb_w_o = nrm((N_A_LAYERS, SB_HEADS * SB_HEAD_DIM, D), (SB_HEADS * SB_HEAD_DIM) ** -0.5)
    gm_w_in = nrm((N_B_LAYERS, D, 2 * GM_HALF), D ** -0.5)
    gm_b_in = nrm((N_B_LAYERS, 2 * GM_HALF), 0.01)
    gm_v_gain = gain((N_B_LAYERS, GM_HALF))
    gm_w_s = nrm((N_B_LAYERS, GM_GROUPS, GM_CHUNK, GM_CHUNK), GM_CHUNK ** -0.5)
    gm_b_s = gain((N_B_LAYERS, GM_GROUPS, GM_CHUNK))
    gm_w_out = nrm((N_B_LAYERS, GM_HALF, D), GM_HALF ** -0.5)
    ssm_w_in = nrm((N_C_LAYERS, D, SSM_PROJ), D ** -0.5)
    ssm_conv_w = nrm((N_C_LAYERS, SSM_CONV, SSM_CONV_DIM), SSM_CONV ** -0.5)
    ssm_conv_b = nrm((N_C_LAYERS, SSM_CONV_DIM), 0.01)
    dt0 = jnp.exp(jax.random.uniform(next(ks), (N_C_LAYERS, SSM_HEADS), jnp.float32,
                                     math.log(1e-3), math.log(1e-1)))
    ssm_dt_bias = dt0 + jnp.log(-jnp.expm1(-dt0))
    ssm_a_log = jnp.log(jax.random.uniform(next(ks), (N_C_LAYERS, SSM_HEADS), jnp.float32, 1.0, 16.0))
    ssm_d = gain((N_C_LAYERS, SSM_HEADS))
    ssm_norm_gain = gain((N_C_LAYERS, SSM_INNER))
    ssm_w_out = nrm((N_C_LAYERS, SSM_INNER, D), SSM_INNER ** -0.5)
    ffn_w_gu = nrm((DEPTH, D, 2 * FFN_HIDDEN), D ** -0.5)
    ffn_w_down = nrm((DEPTH, FFN_HIDDEN, D), FFN_HIDDEN ** -0.5)
    return {
        "x": x, "mix_norm": mix_norm, "ffn_norm": ffn_norm,
        "sb_w_qkv": sb_w_qkv, "sb_q_gain": sb_q_gain, "sb_k_gain": sb_k_gain, "sb_w_o": sb_w_o,
        "gm_w_in": gm_w_in, "gm_b_in": gm_b_in, "gm_v_gain": gm_v_gain, "gm_w_s": gm_w_s,
        "gm_b_s": gm_b_s, "gm_w_out": gm_w_out,
        "ssm_w_in": ssm_w_in, "ssm_conv_w": ssm_conv_w, "ssm_conv_b": ssm_conv_b,
        "ssm_dt_bias": ssm_dt_bias, "ssm_a_log": ssm_a_log, "ssm_d": ssm_d,
        "ssm_norm_gain": ssm_norm_gain, "ssm_w_out": ssm_w_out,
        "ffn_w_gu": ffn_w_gu, "ffn_w_down": ffn_w_down,
    }


def reference(x, mix_norm, ffn_norm,
              sb_w_qkv, sb_q_gain, sb_k_gain, sb_w_o,
              gm_w_in, gm_b_in, gm_v_gain, gm_w_s, gm_b_s, gm_w_out,
              ssm_w_in, ssm_conv_w, ssm_conv_b, ssm_dt_bias, ssm_a_log, ssm_d,
              ssm_norm_gain, ssm_w_out,
              ffn_w_gu, ffn_w_down):
    for i in range(DEPTH):
        h = rmsnorm(x, mix_norm[i])
        kind = i % N_MIXERS
        j = i // N_MIXERS
        if kind == 0:
            m = stick_breaking_attention(h, sb_w_qkv[j], sb_q_gain[j], sb_k_gain[j], sb_w_o[j])
        elif kind == 1:
            m = chunked_gmlp(h, gm_w_in[j], gm_b_in[j], gm_v_gain[j], gm_w_s[j], gm_b_s[j], gm_w_out[j])
        else:
            m = mamba2_mixer(h, ssm_w_in[j], ssm_conv_w[j], ssm_conv_b[j], ssm_dt_bias[j],
                             ssm_a_log[j], ssm_d[j], ssm_norm_gain[j], ssm_w_out[j])
        x = x + m
        x = x + swiglu(rmsnorm(x, ffn_norm[i]), ffn_w_gu[i], ffn_w_down[i])
    return x
```

```python
import functools
import math

import jax
import jax.numpy as jnp
from jax import lax
from jax.experimental import pallas as pl
from jax.experimental.pallas import tpu as pltpu

F32 = jnp.float32
BF16 = jnp.bfloat16
EPS = 1e-6

LANES = 128
V7X_VMEM_BYTES = 64 * 1024 * 1024
VMEM_LIMIT = 56 * 1024 * 1024

HEAD_DIM = 64
CHUNK = 128
SSM_HEAD_DIM = 64
SSM_STATE = 128
SSM_GROUPS = 8
SSM_CONV = 4
GM_GROUPS = 16
SB_DEAD_LOG = -110.0


def _cparams(sem):
    return pltpu.CompilerParams(dimension_semantics=sem, vmem_limit_bytes=VMEM_LIMIT)


def _resident(shape):
    zeros = (0,) * len(shape)
    return pl.BlockSpec(shape, lambda *_: zeros, pipeline_mode=pl.Buffered(1))


def _rmsnorm(x, g):
    ms = jnp.mean(x * x, axis=-1, keepdims=True)
    return x * lax.rsqrt(ms + EPS) * g


def _split_bf16(x, terms):
    parts = []
    r = x
    for _ in range(terms):
        p = r.astype(BF16)
        parts.append(p)
        r = r - p.astype(F32)
    return parts


def _dot(a, b):
    return jnp.dot(a, b, preferred_element_type=F32)


def _dot_nt(a, b):
    return lax.dot_general(a, b, (((1,), (1,)), ((), ())), preferred_element_type=F32)


def _silu(x):
    return x * jax.nn.sigmoid(x)


def _softplus(x):
    return jnp.maximum(x, 0.0) + jnp.log1p(jnp.exp(-jnp.abs(x)))


def _norm_matmul_kernel(x_ref, g_ref, w_ref, o_ref, h_ref):
    @pl.when(pl.program_id(1) == 0)
    def _():
        h_ref[...] = _rmsnorm(x_ref[...], g_ref[...]).astype(BF16)

    o_ref[...] = _dot(h_ref[...], w_ref[...]).astype(o_ref.dtype)


def _norm_matmul(x, g, w, out_dtype, tm, tn):
    m, k = x.shape
    n = w.shape[1]
    return pl.pallas_call(
        _norm_matmul_kernel,
        grid=(m // tm, n // tn),
        in_specs=[pl.BlockSpec((tm, k), lambda i, j: (i, 0)),
                  pl.BlockSpec((1, k), lambda i, j: (0, 0)),
                  pl.BlockSpec((k, tn), lambda i, j: (0, j))],
        out_specs=pl.BlockSpec((tm, tn), lambda i, j: (i, j)),
        out_shape=jax.ShapeDtypeStruct((m, n), out_dtype),
        scratch_shapes=[pltpu.VMEM((tm, k), BF16)],
        compiler_params=_cparams(("parallel", "arbitrary")),
        name="norm_matmul",
    )(x, g, w)


def _tail_kernel(m_ref, wp_ref, x_ref, g_ref, wgu_ref, wd_ref, o_ref, a_ref, *, hidden, th):
    x1 = x_ref[...] + _dot(m_ref[...], wp_ref[...])
    h = _rmsnorm(x1, g_ref[...]).astype(BF16)
    for c in range(hidden // th):
        gate = _dot(h, wgu_ref[:, c * th:(c + 1) * th])
        up = _dot(h, wgu_ref[:, hidden + c * th:hidden + (c + 1) * th])
        a_ref[:, c * th:(c + 1) * th] = (_silu(gate) * up).astype(BF16)
    o_ref[...] = x1 + _dot(a_ref[...], wd_ref[...])


def _tail(m, w_proj, x, g, w_gu, w_down, tm, th):
    rows, d = x.shape
    kin = m.shape[1]
    hidden = w_down.shape[0]
    return pl.pallas_call(
        functools.partial(_tail_kernel, hidden=hidden, th=th),
        grid=(rows // tm,),
        in_specs=[pl.BlockSpec((tm, kin), lambda i: (i, 0)),
                  _resident((kin, d)),
                  pl.BlockSpec((tm, d), lambda i: (i, 0)),
                  _resident((1, d)),
                  _resident((d, 2 * hidden)),
                  _resident((hidden, d))],
        out_specs=pl.BlockSpec((tm, d), lambda i: (i, 0)),
        out_shape=jax.ShapeDtypeStruct((rows, d), F32),
        scratch_shapes=[pltpu.VMEM((tm, hidden), BF16)],
        compiler_params=_cparams(("parallel",)),
        name="tail",
    )(m, w_proj, x, g, w_gu, w_down)


def _sb_kernel(q_ref, k_ref, v_ref, qg_ref, kg_ref, o_ref, kn_ref, r_ref, acc_ref, *, seq, nq):
    blk = CHUNK
    lane = lax.broadcasted_iota(jnp.int32, (1, LANES), 1)
    lo = lane < HEAD_DIM

    def headnorm(x, gain):
        x2 = x * x
        s_lo = jnp.sum(jnp.where(lo, x2, 0.0), axis=1, keepdims=True)
        s_hi = jnp.sum(jnp.where(lo, 0.0, x2), axis=1, keepdims=True)
        ms = jnp.where(lo, s_lo, s_hi) * (1.0 / HEAD_DIM)
        return x * lax.rsqrt(ms + EPS) * gain

    knorm_rows = 256

    def knorm_body(r, carry):
        rows = pl.ds(pl.multiple_of(r * knorm_rows, knorm_rows), knorm_rows)
        kn_ref[rows, :] = headnorm(k_ref[0, rows, :].astype(F32), kg_ref[...]).astype(BF16)
        return carry

    lax.fori_loop(0, seq // knorm_rows, knorm_body, 0)

    row2 = lax.broadcasted_iota(jnp.int32, (2 * blk, blk), 0)
    col2 = lax.broadcasted_iota(jnp.int32, (2 * blk, blk), 1)
    col_minus_row = col2 - jnp.where(row2 >= blk, row2 - blk, row2)
    rj = lax.broadcasted_iota(jnp.int32, (blk, 2 * blk), 0)
    cs = lax.broadcasted_iota(jnp.int32, (blk, 2 * blk), 1)
    suffix_total = jnp.where((rj > cs) | (cs >= blk), 1.0, 0.0).astype(BF16)
    scale = 1.0 / math.sqrt(HEAD_DIM)

    def group_body(g, carry):
        q_stack = []
        for j in range(nq):
            rows = pl.ds(pl.multiple_of((g * nq + j) * blk, blk), blk)
            qn = headnorm(q_ref[0, rows, :].astype(F32), qg_ref[...]) * scale
            q_stack.append(jnp.concatenate(
                [jnp.where(lo, qn, 0.0).astype(BF16), jnp.where(lo, 0.0, qn).astype(BF16)], axis=0))
            r_ref[j] = jnp.zeros((2 * blk, blk), F32)
            acc_ref[j] = jnp.zeros((blk, LANES), F32)
        last_block = g * nq + nq - 1

        def cond(c):
            i, max_r = c
            return (i <= last_block) & (max_r > SB_DEAD_LOG)

        def body(c):
            i, _ = c
            max_r = None
            for j in range(nq):
                kb = g * nq + j - i
                krows = pl.ds(pl.multiple_of(jnp.maximum(kb, 0) * blk, blk), blk)
                kt = kn_ref[krows, :]
                vt = v_ref[0, krows, :]
                offs = jnp.where(i > 0, blk, 0) - jnp.where(kb < 0, 2 * blk, 0)
                valid = col_minus_row < offs
                z = _dot_nt(q_stack[j], kt)
                soft = jnp.log1p(jnp.exp(-jnp.abs(z)))
                log_beta = jnp.minimum(z, 0.0) - soft
                log_1m = jnp.where(valid, -jnp.maximum(z, 0.0) - soft, 0.0)
                hi, lo_part = _split_bf16(log_1m, 2)
                st = _dot(hi, suffix_total) + _dot(lo_part, suffix_total)
                r_old = r_ref[j]
                w = jnp.where(valid, jnp.exp(log_beta + st[:, :blk] + r_old), 0.0).astype(BF16)
                r_new = r_old + st[:, blk:]
                r_ref[j] = r_new
                w_pair = jnp.concatenate([w[:blk], w[blk:]], axis=1)
                v_pair = jnp.concatenate([jnp.where(lo, vt, 0), jnp.where(lo, 0, vt)], axis=0)
                acc_ref[j] += _dot(w_pair, v_pair.astype(BF16))
                max_r = r_new if max_r is None else jnp.maximum(max_r, r_new)
            return i + 1, jnp.max(max_r)

        lax.while_loop(cond, body, (jnp.int32(0), jnp.float32(0.0)))
        for j in range(nq):
            rows = pl.ds(pl.multiple_of((g * nq + j) * blk, blk), blk)
            o_ref[0, rows, :] = acc_ref[j].astype(o_ref.dtype)
        return carry

    lax.fori_loop(0, seq // (blk * nq), group_body, 0)


def _sb_attention(qkv, q_gain, k_gain, nq=4):
    b, seq, three_d = qkv.shape
    d = three_d // 3
    pairs = d // LANES
    qg = jnp.tile(q_gain.astype(F32), 2).reshape(1, LANES)
    kg = jnp.tile(k_gain.astype(F32), 2).reshape(1, LANES)
    blk = CHUNK
    return pl.pallas_call(
        functools.partial(_sb_kernel, seq=seq, nq=nq),
        grid=(b, pairs),
        in_specs=[pl.BlockSpec((1, seq, LANES), lambda i, p: (i, 0, p)),
                  pl.BlockSpec((1, seq, LANES), lambda i, p: (i, 0, pairs + p)),
                  pl.BlockSpec((1, seq, LANES), lambda i, p: (i, 0, 2 * pairs + p)),
                  pl.BlockSpec((1, LANES), lambda i, p: (0, 0)),
                  pl.BlockSpec((1, LANES), lambda i, p: (0, 0))],
        out_specs=pl.BlockSpec((1, seq, LANES), lambda i, p: (i, 0, p)),
        out_shape=jax.ShapeDtypeStruct((b, seq, d), BF16),
        scratch_shapes=[pltpu.VMEM((seq, LANES), BF16),
                        pltpu.VMEM((nq, 2 * blk, blk), F32),
                        pltpu.VMEM((nq, blk, LANES), F32)],
        compiler_params=_cparams(("parallel", "parallel")),
        name="sb_attention",
    )(qkv, qkv, qkv, qg, kg)


def _gelu(x):
    return 0.5 * x * (1.0 + lax.erf(x * (1.0 / math.sqrt(2.0))))


def _gmlp_kernel(x_ref, g_ref, w_ref, b_ref, vg_ref, ws_ref, bs_ref, o_ref, u_ref, vn_ref, *, half):
    tm = x_ref.shape[0]
    h = _rmsnorm(x_ref[...], g_ref[...]).astype(BF16)
    u_ref[...] = _gelu(_dot(h, w_ref[:, :half]) + b_ref[:, :half])
    v = _gelu(_dot(h, w_ref[:, half:]) + b_ref[:, half:])
    vn_ref[...] = _rmsnorm(v, vg_ref[...]).astype(BF16)
    t_idx = lax.broadcasted_iota(jnp.int32, (CHUNK, CHUNK), 0)
    s_idx = lax.broadcasted_iota(jnp.int32, (CHUNK, CHUNK), 1)
    causal = t_idx >= s_idx
    for grp in range(half // LANES):
        cols = slice(grp * LANES, (grp + 1) * LANES)
        w_s = jnp.where(causal, ws_ref[grp], 0.0).astype(BF16)
        for c in range(tm // CHUNK):
            rows = slice(c * CHUNK, (c + 1) * CHUNK)
            mixed = _dot(w_s, vn_ref[rows, cols]) + bs_ref[:, cols]
            o_ref[rows, cols] = (u_ref[rows, cols] * mixed).astype(o_ref.dtype)


def _gmlp_front(x, g, w_in, b_in, v_gain, w_s, bs_full, tm):
    rows, d = x.shape
    half = w_in.shape[1] // 2
    groups = w_s.shape[0]
    return pl.pallas_call(
        functools.partial(_gmlp_kernel, half=half),
        grid=(rows // tm,),
        in_specs=[pl.BlockSpec((tm, d), lambda i: (i, 0)),
                  _resident((1, d)),
                  _resident((d, 2 * half)),
                  _resident((1, 2 * half)),
                  _resident((1, half)),
                  _resident((groups, CHUNK, CHUNK)),
                  _resident((CHUNK, half))],
        out_specs=pl.BlockSpec((tm, half), lambda i: (i, 0)),
        out_shape=jax.ShapeDtypeStruct((rows, half), BF16),
        scratch_shapes=[pltpu.VMEM((tm, half), F32), pltpu.VMEM((tm, half), BF16)],
        compiler_params=_cparams(("parallel",)),
        name="gmlp_front",
    )(x, g, w_in, b_in, v_gain, w_s, bs_full)


def _ssd_kernel(z_ref, xs_ref, bc_ref, dt_ref, cwx_ref, cwb_ref, cbx_ref, cbb_ref, dtb_ref, alog_ref,
                dexp_ref, ng_ref, o_ref, extx_ref, extb_ref, state_ref, y_ref, *, inner):
    L = CHUNK
    halo = 8
    gstate = SSM_GROUPS * SSM_STATE
    heads_per_group = inner // SSM_HEAD_DIM // SSM_GROUPS
    pairs_per_group = heads_per_group // 2

    @pl.when(pl.program_id(1) == 0)
    def _():
        extx_ref[0:halo, :] = jnp.zeros((halo, inner), F32)
        extb_ref[0:halo, :] = jnp.zeros((halo, 2 * gstate), F32)
        state_ref[...] = jnp.zeros(state_ref.shape, F32)

    extx_ref[halo:halo + L, :] = xs_ref[...].astype(F32)
    extb_ref[halo:halo + L, :] = bc_ref[...].astype(F32)

    def conv_silu(ext_ref, w_ref, b_ref):
        acc = b_ref[...]
        for k in range(SSM_CONV):
            start = halo - (SSM_CONV - 1) + k
            acc = acc + w_ref[k:k + 1, :] * ext_ref[start:start + L, :]
        return _silu(acc)

    xs = conv_silu(extx_ref, cwx_ref, cbx_ref)
    bcm = conv_silu(extb_ref, cwb_ref, cbb_ref)
    extx_ref[0:halo, :] = extx_ref[L:L + halo, :]
    extb_ref[0:halo, :] = extb_ref[L:L + halo, :]

    dt = _softplus(dt_ref[...] + dtb_ref[...])
    a = dt * (-jnp.exp(alog_ref[...]))
    t_idx = lax.broadcasted_iota(jnp.int32, (L, L), 0)
    s_idx = lax.broadcasted_iota(jnp.int32, (L, L), 1)
    causal = t_idx >= s_idx
    tril = jnp.where(causal, 1.0, 0.0).astype(BF16)
    a_cum = sum(_dot(tril, p) for p in _split_bf16(a, 3))
    a_last = a_cum[L - 1:L, :]
    chunk_decay = jnp.exp(a_last)
    wgt = dt * jnp.exp(a_last - a_cum)
    a_cum_t = a_cum.T
    dt_t = dt.T
    exp_a_cum = jnp.exp(a_cum)

    lane = lax.broadcasted_iota(jnp.int32, (1, LANES), 1)
    lo = lane < SSM_HEAD_DIM

    for grp in range(SSM_GROUPS):
        b_g = bcm[:, grp * SSM_STATE:(grp + 1) * SSM_STATE]
        c_g = bcm[:, gstate + grp * SSM_STATE:gstate + (grp + 1) * SSM_STATE]
        cb = _dot_nt(c_g.astype(BF16), b_g.astype(BF16))
        for pr in range(pairs_per_group):
            pair = grp * pairs_per_group + pr
            cols = slice(pair * LANES, (pair + 1) * LANES)
            xs_pair = xs[:, cols]
            prev_pair = state_ref[grp, :, pr * LANES:(pr + 1) * LANES]
            y_pair = None
            st_pair = None
            for e in range(2):
                head = 2 * pair + e
                sel = lo if e == 0 else jnp.logical_not(lo)
                xs_e = jnp.where(sel, xs_pair, 0.0).astype(BF16)
                prev_e = jnp.where(sel, prev_pair, 0.0).astype(BF16)
                a_col = jnp.broadcast_to(a_cum[:, head:head + 1], (L, L))
                a_row = jnp.broadcast_to(a_cum_t[head:head + 1, :], (L, L))
                decay = jnp.exp(jnp.where(causal, a_col - a_row, -jnp.inf))
                dt_row = jnp.broadcast_to(dt_t[head:head + 1, :], (L, L))
                m_h = (cb * decay * dt_row).astype(BF16)
                e_h = (c_g * jnp.broadcast_to(exp_a_cum[:, head:head + 1], (L, SSM_STATE))).astype(BF16)
                y_e = _dot(m_h, xs_e) + _dot(e_h, prev_e)
                b_w = b_g * jnp.broadcast_to(wgt[:, head:head + 1], (L, SSM_STATE))
                st_e = _dot(b_w.T.astype(BF16), xs_e)
                y_pair = y_e if y_pair is None else y_pair + y_e
                st_pair = st_e if st_pair is None else st_pair + st_e
            cd = jnp.where(lo,
                           jnp.broadcast_to(chunk_decay[:, 2 * pair:2 * pair + 1], (1, LANES)),
                           jnp.broadcast_to(chunk_decay[:, 2 * pair + 1:2 * pair + 2], (1, LANES)))
            state_ref[grp, :, pr * LANES:(pr + 1) * LANES] = prev_pair * cd + st_pair
            y_ref[:, cols] = y_pair

    z = z_ref[...].astype(F32)
    yg = (y_ref[...] + xs * dexp_ref[...]) * _silu(z)
    gw = inner // SSM_GROUPS
    for grp in range(SSM_GROUPS):
        cols = slice(grp * gw, (grp + 1) * gw)
        o_ref[:, cols] = _rmsnorm(yg[:, cols], ng_ref[:, cols]).astype(o_ref.dtype)


def _ssd(zxbc, dt_raw, conv_w, conv_b, dt_bias, a_log, d_exp, norm_gain, batch, seq, inner):
    gstate = SSM_GROUPS * SSM_STATE
    nc = seq // CHUNK
    heads = inner // SSM_HEAD_DIM
    cwx, cwb = conv_w[:, :inner], conv_w[:, inner:]
    cbx, cbb = conv_b[:, :inner], conv_b[:, inner:]

    def row_block(col):
        return pl.BlockSpec((CHUNK, inner), lambda b, c: (b * nc + c, col))

    return pl.pallas_call(
        functools.partial(_ssd_kernel, inner=inner),
        grid=(batch, nc),
        in_specs=[row_block(0), row_block(1), row_block(2),
                  pl.BlockSpec((CHUNK, LANES), lambda b, c: (b * nc + c, 0)),
                  _resident((SSM_CONV, inner)), _resident((SSM_CONV, 2 * gstate)),
                  _resident((1, inner)), _resident((1, 2 * gstate)),
                  _resident((1, LANES)), _resident((1, LANES)),
                  _resident((1, inner)), _resident((1, inner))],
        out_specs=pl.BlockSpec((CHUNK, inner), lambda b, c: (b * nc + c, 0)),
        out_shape=jax.ShapeDtypeStruct((batch * seq, inner), BF16),
        scratch_shapes=[pltpu.VMEM((CHUNK + 8, inner), F32),
                        pltpu.VMEM((CHUNK + 8, 2 * gstate), F32),
                        pltpu.VMEM((SSM_GROUPS, SSM_STATE, heads // SSM_GROUPS * SSM_HEAD_DIM), F32),
                        pltpu.VMEM((CHUNK, inner), F32)],
        compiler_params=_cparams(("parallel", "arbitrary")),
        name="ssd",
    )(zxbc, zxbc, zxbc, dt_raw, cwx, cwb, cbx, cbb, dt_bias, a_log, d_exp, norm_gain)


def kernel(x, mix_norm, ffn_norm, sb_w_qkv, sb_q_gain, sb_k_gain, sb_w_o, gm_w_in, gm_b_in, gm_v_gain,
           gm_w_s, gm_b_s, gm_w_out, ssm_w_in, ssm_conv_w, ssm_conv_b, ssm_dt_bias, ssm_a_log, ssm_d,
           ssm_norm_gain, ssm_w_out, ffn_w_gu, ffn_w_down):
    batch, seq, d = x.shape
    rows = batch * seq
    depth = mix_norm.shape[0]
    xf = x.reshape(rows, d)
    tail_tm, tail_th = 512, 256
    for i in range(depth):
        kind, j = i % 3, i // 3
        g_mix = mix_norm[i].reshape(1, d)
        if kind == 0:
            qkv = _norm_matmul(xf, g_mix, sb_w_qkv[j].astype(BF16), BF16, 1024, 1024)
            m = _sb_attention(qkv.reshape(batch, seq, -1), sb_q_gain[j], sb_k_gain[j]).reshape(rows, -1)
            w_proj = sb_w_o[j]
        elif kind == 1:
            half = gm_w_in.shape[2] // 2
            bs_full = jnp.repeat(gm_b_s[j].T, half // GM_GROUPS, axis=1)
            m = _gmlp_front(xf, g_mix, gm_w_in[j].astype(BF16), gm_b_in[j].reshape(1, -1),
                            gm_v_gain[j].reshape(1, -1), gm_w_s[j], bs_full, 256)
            w_proj = gm_w_out[j]
        else:
            inner = ssm_w_out.shape[1]
            heads = ssm_dt_bias.shape[1]
            conv_dim = ssm_conv_w.shape[2]
            w_in = ssm_w_in[j]
            zxbc = _norm_matmul(xf, g_mix, w_in[:, :inner + conv_dim].astype(BF16), BF16, 1024, 1024)
            w_dt = jnp.pad(w_in[:, inner + conv_dim:], ((0, 0), (0, LANES - heads))).astype(BF16)
            dt_raw = _norm_matmul(xf, g_mix, w_dt, F32, 1024, LANES)
            pad_h = (0, LANES - heads)
            m = _ssd(zxbc, dt_raw, ssm_conv_w[j], ssm_conv_b[j].reshape(1, -1),
                     jnp.pad(ssm_dt_bias[j], pad_h).reshape(1, LANES),
                     jnp.pad(ssm_a_log[j], pad_h).reshape(1, LANES),
                     jnp.repeat(ssm_d[j], SSM_HEAD_DIM).reshape(1, inner),
                     ssm_norm_gain[j].reshape(1, inner), batch, seq, inner)
            w_proj = ssm_w_out[j]
        xf = _tail(m, w_proj.astype(BF16), xf, ffn_norm[i].reshape(1, d),
                   ffn_w_gu[i].astype(BF16), ffn_w_down[i].astype(BF16), tail_tm, tail_th)
    return xf.reshape(batch, seq, d)
```

```python
import functools
import math

import jax
import jax.numpy as jnp
from jax import lax
from jax.experimental import pallas as pl
from jax.experimental.pallas import tpu as pltpu

F32 = jnp.float32
BF16 = jnp.bfloat16
EPS = 1e-6
LOG2E = 1.4426950408889634

LANES = 128
V7X_VMEM_BYTES = 64 * 1024 * 1024
VMEM_LIMIT = 56 * 1024 * 1024

HEAD_DIM = 64
CHUNK = 128
SSM_HEAD_DIM = 64
SSM_STATE = 128
SSM_GROUPS = 8
SSM_CONV = 4
GM_GROUPS = 16
SB_DEAD_LOG = -110.0


def _cparams(sem):
    return pltpu.CompilerParams(dimension_semantics=sem, vmem_limit_bytes=VMEM_LIMIT)


def _resident(shape):
    zeros = (0,) * len(shape)
    return pl.BlockSpec(shape, lambda *_: zeros, pipeline_mode=pl.Buffered(1))


def _rmsnorm(x, g):
    ms = jnp.mean(x * x, axis=-1, keepdims=True)
    return x * lax.rsqrt(ms + EPS) * g


def _split_bf16(x, terms):
    parts = []
    r = x
    for _ in range(terms):
        p = r.astype(BF16)
        parts.append(p)
        r = r - p.astype(F32)
    return parts


def _dot(a, b):
    return jnp.dot(a, b, preferred_element_type=F32)


def _dot_nt(a, b):
    return lax.dot_general(a, b, (((1,), (1,)), ((), ())), preferred_element_type=F32)


def _silu(x):
    return x * jax.nn.sigmoid(x)


def _softplus(x):
    return jnp.maximum(x, 0.0) + jnp.log1p(jnp.exp(-jnp.abs(x)))


def _norm_matmul_kernel(x_ref, g_ref, w_ref, o_ref, h_ref):
    @pl.when(pl.program_id(1) == 0)
    def _():
        h_ref[...] = _rmsnorm(x_ref[...], g_ref[...]).astype(BF16)

    o_ref[...] = _dot(h_ref[...], w_ref[...]).astype(o_ref.dtype)


def _norm_matmul(x, g, w, out_dtype, tm, tn):
    m, k = x.shape
    n = w.shape[1]
    return pl.pallas_call(
        _norm_matmul_kernel,
        grid=(m // tm, n // tn),
        in_specs=[pl.BlockSpec((tm, k), lambda i, j: (i, 0)),
                  pl.BlockSpec((1, k), lambda i, j: (0, 0)),
                  pl.BlockSpec((k, tn), lambda i, j: (0, j))],
        out_specs=pl.BlockSpec((tm, tn), lambda i, j: (i, j)),
        out_shape=jax.ShapeDtypeStruct((m, n), out_dtype),
        scratch_shapes=[pltpu.VMEM((tm, k), BF16)],
        compiler_params=_cparams(("parallel", "arbitrary")),
        name="norm_matmul",
    )(x, g, w)


def _tail_kernel(m_ref, wp_ref, x_ref, g_ref, wgu_ref, wd_ref, o_ref, a_ref, *, hidden, th):
    x1 = x_ref[...] + _dot(m_ref[...], wp_ref[...])
    h = _rmsnorm(x1, g_ref[...]).astype(BF16)
    for c in range(hidden // th):
        gate = _dot(h, wgu_ref[:, c * th:(c + 1) * th])
        up = _dot(h, wgu_ref[:, hidden + c * th:hidden + (c + 1) * th])
        a_ref[:, c * th:(c + 1) * th] = (_silu(gate) * up).astype(BF16)
    o_ref[...] = x1 + _dot(a_ref[...], wd_ref[...])


def _tail(m, w_proj, x, g, w_gu, w_down, tm, th):
    rows, d = x.shape
    kin = m.shape[1]
    hidden = w_down.shape[0]
    return pl.pallas_call(
        functools.partial(_tail_kernel, hidden=hidden, th=th),
        grid=(rows // tm,),
        in_specs=[pl.BlockSpec((tm, kin), lambda i: (i, 0)),
                  _resident((kin, d)),
                  pl.BlockSpec((tm, d), lambda i: (i, 0)),
                  _resident((1, d)),
                  _resident((d, 2 * hidden)),
                  _resident((hidden, d))],
        out_specs=pl.BlockSpec((tm, d), lambda i: (i, 0)),
        out_shape=jax.ShapeDtypeStruct((rows, d), F32),
        scratch_shapes=[pltpu.VMEM((tm, hidden), BF16)],
        compiler_params=_cparams(("parallel",)),
        name="tail",
    )(m, w_proj, x, g, w_gu, w_down)


SB_FAST_BLOCKS = 3


def _sb_kernel(q_ref, k_ref, v_ref, qg_ref, kg_ref, o_ref, qn_ref, k2_ref, v2_ref, r_ref, acc_ref,
               lb_ref, lhs_ref, st_ref, w_ref, *, seq, nq):
    blk = CHUNK
    two = 2 * blk
    fast = SB_FAST_BLOCKS
    lane = lax.broadcasted_iota(jnp.int32, (1, LANES), 1)
    lo = lane < HEAD_DIM

    hk = lax.broadcasted_iota(jnp.int32, (two, LANES), 0)
    hn = lax.broadcasted_iota(jnp.int32, (two, LANES), 1)
    head_ones = jnp.where(((hk & (LANES - 1)) < HEAD_DIM) == (hn < HEAD_DIM), 1.0, 0.0).astype(BF16)

    q_scale = LOG2E / math.sqrt(HEAD_DIM)

    def prep_body(r, carry):
        rows = pl.ds(pl.multiple_of(r * two, two), two)
        kq = jnp.concatenate([k_ref[0, rows, :], q_ref[0, rows, :]], axis=0).astype(F32)
        hi, lo_part = _split_bf16(kq * kq, 2)
        ms = _dot(jnp.concatenate([hi, lo_part], axis=1), head_ones) * (1.0 / HEAD_DIM)
        kq = kq * lax.rsqrt(ms + EPS)
        kn = kq[:two] * kg_ref[...]
        qn_ref[rows, :] = (kq[two:] * (qg_ref[...] * q_scale)).astype(BF16)
        vv = v_ref[0, rows, :].astype(F32)
        for half in range(2):
            sl = slice(half * blk, (half + 1) * blk)
            head0 = pl.ds(pl.multiple_of((2 * r + half) * two, two), blk)
            head1 = pl.ds(pl.multiple_of((2 * r + half) * two + blk, blk), blk)
            k2_ref[head0, :] = jnp.where(lo, kn[sl], 0.0).astype(BF16)
            k2_ref[head1, :] = jnp.where(lo, 0.0, kn[sl]).astype(BF16)
            v2_ref[head0, :] = jnp.where(lo, vv[sl], 0.0).astype(BF16)
            v2_ref[head1, :] = jnp.where(lo, 0.0, vv[sl]).astype(BF16)
        return carry

    lax.fori_loop(0, seq // two, prep_body, 0)

    uj = lax.broadcasted_iota(jnp.int32, (two, two), 0) & (blk - 1)
    us = lax.broadcasted_iota(jnp.int32, (two, two), 1)
    suffix_total = jnp.where((uj > us) | (us >= blk), 1.0, 0.0).astype(BF16)
    key_minus_query = ((lax.broadcasted_iota(jnp.int32, (blk, two), 1) & (blk - 1))
                       - lax.broadcasted_iota(jnp.int32, (blk, two), 0))
    dead_log2 = SB_DEAD_LOG * LOG2E

    def log_sigmoids(z):
        log_beta = jnp.minimum(z, 0.0) - jnp.log(1.0 + jnp.exp2(-jnp.abs(z))) * LOG2E
        return log_beta, log_beta - z

    def hi_lo(x):
        hi, lo_part = _split_bf16(x, 2)
        return jnp.concatenate([hi, lo_part], axis=1)

    def suffix_and_total(st0, st1):
        return (jnp.concatenate([st0[:, :blk], st1[:, :blk]], axis=1),
                jnp.concatenate([st0[:, blk:], st1[:, blk:]], axis=1))

    def block(qb, kb, valid, r_in):
        krows = pl.ds(pl.multiple_of(kb * two, two), two)
        log_beta, log_1m = log_sigmoids(_dot_nt(qb, k2_ref[krows, :]))
        log_1m = jnp.where(valid, log_1m, 0.0)
        suffix, total = suffix_and_total(*[_dot(hi_lo(log_1m[:, h * blk:(h + 1) * blk]), suffix_total)
                                           for h in range(2)])
        w = jnp.where(valid, jnp.exp2(log_beta + suffix + r_in), 0.0)
        return _dot(w.astype(BF16), v2_ref[krows, :]), total

    diag_valid = key_minus_query < 0

    def fast_blocks(g, qbs):
        def key_rows(j):
            return pl.ds(pl.multiple_of((g * nq + j - (fast - 1)) * two, two), fast * two)

        for j in range(nq):
            log_beta, log_1m = log_sigmoids(_dot_nt(qbs[j], k2_ref[key_rows(j), :]))
            lb_ref[j] = log_beta
            for i in range(fast):
                c0 = (fast - 1 - i) * two
                l1 = log_1m[:, c0:c0 + two]
                if i == 0:
                    l1 = jnp.where(diag_valid, l1, 0.0)
                for h in range(2):
                    row0 = ((j * fast + i) * 2 + h) * blk
                    lhs_ref[row0:row0 + blk, :] = hi_lo(l1[:, h * blk:(h + 1) * blk])
        st_ref[...] = _dot(lhs_ref[...], suffix_total)
        for j in range(nq):
            r_run = None
            for i in range(fast):
                c0 = (fast - 1 - i) * two
                row0 = (j * fast + i) * two
                suffix, total = suffix_and_total(st_ref[row0:row0 + blk, :], st_ref[row0 + blk:row0 + two, :])
                arg = lb_ref[j, :, c0:c0 + two] + suffix
                if r_run is not None:
                    arg = arg + r_run
                w = jnp.exp2(arg)
                if i == 0:
                    w = jnp.where(diag_valid, w, 0.0)
                w_ref[j, :, c0:c0 + two] = w.astype(BF16)
                r_run = total if r_run is None else r_run + total
            r_ref[j] = r_run
            acc_ref[j] = _dot(w_ref[j], v2_ref[key_rows(j), :])

    def group_body(g, carry):
        qbs = []
        for j in range(nq):
            rows = pl.ds(pl.multiple_of((g * nq + j) * blk, blk), blk)
            qbs.append(qn_ref[rows, :])

        @pl.when(g == 0)
        def _():
            for j in range(nq):
                r_ref[j] = jnp.zeros((blk, two), F32)
                acc_ref[j] = jnp.zeros((blk, LANES), F32)

        @pl.when(g > 0)
        def _():
            fast_blocks(g, qbs)

        def max_r():
            m = r_ref[0]
            for j in range(1, nq):
                m = jnp.maximum(m, r_ref[j])
            return jnp.max(m)

        last_block = g * nq + nq - 1

        def cond(c):
            i, m = c
            return (i <= last_block) & (m > dead_log2)

        def body(c):
            i, _ = c
            for j in range(nq):
                kb = g * nq + j - i
                offs = jnp.where(i > 0, blk, 0) - jnp.where(kb < 0, two, 0)
                pv, tot = block(qbs[j], jnp.maximum(kb, 0), key_minus_query < offs, r_ref[j])
                acc_ref[j] += pv
                r_ref[j] += tot
            return i + 1, max_r()

        lax.while_loop(cond, body, (jnp.where(g > 0, SB_FAST_BLOCKS, 0), max_r()))
        for j in range(nq):
            rows = pl.ds(pl.multiple_of((g * nq + j) * blk, blk), blk)
            o_ref[0, rows, :] = acc_ref[j].astype(o_ref.dtype)
        return carry

    lax.fori_loop(0, seq // (blk * nq), group_body, 0)


def _sb_attention(qkv, q_gain, k_gain, nq=4):
    assert nq >= SB_FAST_BLOCKS - 1
    b, seq, three_d = qkv.shape
    d = three_d // 3
    pairs = d // LANES
    qg = jnp.tile(q_gain.astype(F32), 2).reshape(1, LANES)
    kg = jnp.tile(k_gain.astype(F32), 2).reshape(1, LANES)
    blk = CHUNK
    return pl.pallas_call(
        functools.partial(_sb_kernel, seq=seq, nq=nq),
        grid=(b, pairs),
        in_specs=[pl.BlockSpec((1, seq, LANES), lambda i, p: (i, 0, p)),
                  pl.BlockSpec((1, seq, LANES), lambda i, p: (i, 0, pairs + p)),
                  pl.BlockSpec((1, seq, LANES), lambda i, p: (i, 0, 2 * pairs + p)),
                  pl.BlockSpec((1, LANES), lambda i, p: (0, 0)),
                  pl.BlockSpec((1, LANES), lambda i, p: (0, 0))],
        out_specs=pl.BlockSpec((1, seq, LANES), lambda i, p: (i, 0, p)),
        out_shape=jax.ShapeDtypeStruct((b, seq, d), BF16),
        scratch_shapes=[pltpu.VMEM((seq, LANES), BF16),
                        pltpu.VMEM((2 * seq, LANES), BF16),
                        pltpu.VMEM((2 * seq, LANES), BF16),
                        pltpu.VMEM((nq, blk, 2 * blk), F32),
                        pltpu.VMEM((nq, blk, LANES), F32),
                        pltpu.VMEM((nq, blk, SB_FAST_BLOCKS * 2 * blk), F32),
                        pltpu.VMEM((nq * SB_FAST_BLOCKS * 2 * blk, 2 * blk), BF16),
                        pltpu.VMEM((nq * SB_FAST_BLOCKS * 2 * blk, 2 * blk), F32),
                        pltpu.VMEM((nq, blk, SB_FAST_BLOCKS * 2 * blk), BF16)],

        compiler_params=_cparams(("parallel", "parallel")),
        name="sb_attention",
    )(qkv, qkv, qkv, qg, kg)


def _gelu(x):
    return 0.5 * x * (1.0 + lax.erf(x * (1.0 / math.sqrt(2.0))))


def _gmlp_kernel(x_ref, g_ref, w_ref, b_ref, vg_ref, ws_ref, bs_ref, o_ref, u_ref, vn_ref, *, half):
    tm = x_ref.shape[0]
    h = _rmsnorm(x_ref[...], g_ref[...]).astype(BF16)
    u_ref[...] = _gelu(_dot(h, w_ref[:, :half]) + b_ref[:, :half])
    v = _gelu(_dot(h, w_ref[:, half:]) + b_ref[:, half:])
    vn_ref[...] = _rmsnorm(v, vg_ref[...]).astype(BF16)
    t_idx = lax.broadcasted_iota(jnp.int32, (CHUNK, CHUNK), 0)
    s_idx = lax.broadcasted_iota(jnp.int32, (CHUNK, CHUNK), 1)
    causal = t_idx >= s_idx
    for grp in range(half // LANES):
        cols = slice(grp * LANES, (grp + 1) * LANES)
        w_s = jnp.where(causal, ws_ref[grp], 0.0).astype(BF16)
        for c in range(tm // CHUNK):
            rows = slice(c * CHUNK, (c + 1) * CHUNK)
            mixed = _dot(w_s, vn_ref[rows, cols]) + bs_ref[:, cols]
            o_ref[rows, cols] = (u_ref[rows, cols] * mixed).astype(o_ref.dtype)


def _gmlp_front(x, g, w_in, b_in, v_gain, w_s, bs_full, tm):
    rows, d = x.shape
    half = w_in.shape[1] // 2
    groups = w_s.shape[0]
    return pl.pallas_call(
        functools.partial(_gmlp_kernel, half=half),
        grid=(rows // tm,),
        in_specs=[pl.BlockSpec((tm, d), lambda i: (i, 0)),
                  _resident((1, d)),
                  _resident((d, 2 * half)),
                  _resident((1, 2 * half)),
                  _resident((1, half)),
                  _resident((groups, CHUNK, CHUNK)),
                  _resident((CHUNK, half))],
        out_specs=pl.BlockSpec((tm, half), lambda i: (i, 0)),
        out_shape=jax.ShapeDtypeStruct((rows, half), BF16),
        scratch_shapes=[pltpu.VMEM((tm, half), F32), pltpu.VMEM((tm, half), BF16)],
        compiler_params=_cparams(("parallel",)),
        name="gmlp_front",
    )(x, g, w_in, b_in, v_gain, w_s, bs_full)


def _ssd_kernel(z_ref, xs_ref, bc_ref, dt_ref, cwx_ref, cwb_ref, cbx_ref, cbb_ref, dtb_ref, alog_ref,
                dexp_ref, ng_ref, o_ref, extx_ref, extb_ref, state_ref, y_ref, *, inner):
    L = CHUNK
    halo = 8
    gstate = SSM_GROUPS * SSM_STATE
    heads_per_group = inner // SSM_HEAD_DIM // SSM_GROUPS
    pairs_per_group = heads_per_group // 2

    @pl.when(pl.program_id(1) == 0)
    def _():
        extx_ref[0:halo, :] = jnp.zeros((halo, inner), F32)
        extb_ref[0:halo, :] = jnp.zeros((halo, 2 * gstate), F32)
        state_ref[...] = jnp.zeros(state_ref.shape, F32)

    extx_ref[halo:halo + L, :] = xs_ref[...].astype(F32)
    extb_ref[halo:halo + L, :] = bc_ref[...].astype(F32)

    def conv_silu(ext_ref, w_ref, b_ref):
        acc = b_ref[...]
        for k in range(SSM_CONV):
            start = halo - (SSM_CONV - 1) + k
            acc = acc + w_ref[k:k + 1, :] * ext_ref[start:start + L, :]
        return _silu(acc)

    xs = conv_silu(extx_ref, cwx_ref, cbx_ref)
    bcm = conv_silu(extb_ref, cwb_ref, cbb_ref)
    extx_ref[0:halo, :] = extx_ref[L:L + halo, :]
    extb_ref[0:halo, :] = extb_ref[L:L + halo, :]

    dt = _softplus(dt_ref[...] + dtb_ref[...])
    a = dt * (-jnp.exp(alog_ref[...]))
    t_idx = lax.broadcasted_iota(jnp.int32, (L, L), 0)
    s_idx = lax.broadcasted_iota(jnp.int32, (L, L), 1)
    causal = t_idx >= s_idx
    tril = jnp.where(causal, 1.0, 0.0).astype(BF16)
    a_cum = sum(_dot(tril, p) for p in _split_bf16(a, 3))
    a_last = a_cum[L - 1:L, :]
    chunk_decay = jnp.exp(a_last)
    wgt = dt * jnp.exp(a_last - a_cum)
    a_cum_t = a_cum.T
    dt_t = dt.T
    exp_a_cum = jnp.exp(a_cum)

    lane = lax.broadcasted_iota(jnp.int32, (1, LANES), 1)
    lo = lane < SSM_HEAD_DIM

    for grp in range(SSM_GROUPS):
        b_g = bcm[:, grp * SSM_STATE:(grp + 1) * SSM_STATE]
        c_g = bcm[:, gstate + grp * SSM_STATE:gstate + (grp + 1) * SSM_STATE]
        cb = _dot_nt(c_g.astype(BF16), b_g.astype(BF16))
        for pr in range(pairs_per_group):
            pair = grp * pairs_per_group + pr
            cols = slice(pair * LANES, (pair + 1) * LANES)
            xs_pair = xs[:, cols]
            prev_pair = state_ref[grp, :, pr * LANES:(pr + 1) * LANES]
            y_pair = None
            st_pair = None
            for e in range(2):
                head = 2 * pair + e
                sel = lo if e == 0 else jnp.logical_not(lo)
                xs_e = jnp.where(sel, xs_pair, 0.0).astype(BF16)
                prev_e = jnp.where(sel, prev_pair, 0.0).astype(BF16)
                a_col = jnp.broadcast_to(a_cum[:, head:head + 1], (L, L))
                a_row = jnp.broadcast_to(a_cum_t[head:head + 1, :], (L, L))
                decay = jnp.exp(jnp.where(causal, a_col - a_row, -jnp.inf))
                dt_row = jnp.broadcast_to(dt_t[head:head + 1, :], (L, L))
                m_h = (cb * decay * dt_row).astype(BF16)
                e_h = (c_g * jnp.broadcast_to(exp_a_cum[:, head:head + 1], (L, SSM_STATE))).astype(BF16)
                y_e = _dot(m_h, xs_e) + _dot(e_h, prev_e)
                b_w = b_g * jnp.broadcast_to(wgt[:, head:head + 1], (L, SSM_STATE))
                st_e = _dot(b_w.T.astype(BF16), xs_e)
                y_pair = y_e if y_pair is None else y_pair + y_e
                st_pair = st_e if st_pair is None else st_pair + st_e
            cd = jnp.where(lo,
                           jnp.broadcast_to(chunk_decay[:, 2 * pair:2 * pair + 1], (1, LANES)),
                           jnp.broadcast_to(chunk_decay[:, 2 * pair + 1:2 * pair + 2], (1, LANES)))
            state_ref[grp, :, pr * LANES:(pr + 1) * LANES] = prev_pair * cd + st_pair
            y_ref[:, cols] = y_pair

    z = z_ref[...].astype(F32)
    yg = (y_ref[...] + xs * dexp_ref[...]) * _silu(z)
    gw = inner // SSM_GROUPS
    for grp in range(SSM_GROUPS):
        cols = slice(grp * gw, (grp + 1) * gw)
        o_ref[:, cols] = _rmsnorm(yg[:, cols], ng_ref[:, cols]).astype(o_ref.dtype)


def _ssd(zxbc, dt_raw, conv_w, conv_b, dt_bias, a_log, d_exp, norm_gain, batch, seq, inner):
    gstate = SSM_GROUPS * SSM_STATE
    nc = seq // CHUNK
    heads = inner // SSM_HEAD_DIM
    cwx, cwb = conv_w[:, :inner], conv_w[:, inner:]
    cbx, cbb = conv_b[:, :inner], conv_b[:, inner:]

    def row_block(col):
        return pl.BlockSpec((CHUNK, inner), lambda b, c: (b * nc + c, col))

    return pl.pallas_call(
        functools.partial(_ssd_kernel, inner=inner),
        grid=(batch, nc),
        in_specs=[row_block(0), row_block(1), row_block(2),
                  pl.BlockSpec((CHUNK, LANES), lambda b, c: (b * nc + c, 0)),
                  _resident((SSM_CONV, inner)), _resident((SSM_CONV, 2 * gstate)),
                  _resident((1, inner)), _resident((1, 2 * gstate)),
                  _resident((1, LANES)), _resident((1, LANES)),
                  _resident((1, inner)), _resident((1, inner))],
        out_specs=pl.BlockSpec((CHUNK, inner), lambda b, c: (b * nc + c, 0)),
        out_shape=jax.ShapeDtypeStruct((batch * seq, inner), BF16),
        scratch_shapes=[pltpu.VMEM((CHUNK + 8, inner), F32),
                        pltpu.VMEM((CHUNK + 8, 2 * gstate), F32),
                        pltpu.VMEM((SSM_GROUPS, SSM_STATE, heads // SSM_GROUPS * SSM_HEAD_DIM), F32),
                        pltpu.VMEM((CHUNK, inner), F32)],
        compiler_params=_cparams(("parallel", "arbitrary")),
        name="ssd",
    )(zxbc, zxbc, zxbc, dt_raw, cwx, cwb, cbx, cbb, dt_bias, a_log, d_exp, norm_gain)


def kernel(x, mix_norm, ffn_norm, sb_w_qkv, sb_q_gain, sb_k_gain, sb_w_o, gm_w_in, gm_b_in, gm_v_gain,
           gm_w_s, gm_b_s, gm_w_out, ssm_w_in, ssm_conv_w, ssm_conv_b, ssm_dt_bias, ssm_a_log, ssm_d,
           ssm_norm_gain, ssm_w_out, ffn_w_gu, ffn_w_down):
    batch, seq, d = x.shape
    rows = batch * seq
    depth = mix_norm.shape[0]
    xf = x.reshape(rows, d)
    tail_tm, tail_th = 512, 256
    for i in range(depth):
        kind, j = i % 3, i // 3
        g_mix = mix_norm[i].reshape(1, d)
        if kind == 0:
            qkv = _norm_matmul(xf, g_mix, sb_w_qkv[j].astype(BF16), BF16, 1024, 1024)
            m = _sb_attention(qkv.reshape(batch, seq, -1), sb_q_gain[j], sb_k_gain[j]).reshape(rows, -1)
            w_proj = sb_w_o[j]
        elif kind == 1:
            half = gm_w_in.shape[2] // 2
            bs_full = jnp.repeat(gm_b_s[j].T, half // GM_GROUPS, axis=1)
            m = _gmlp_front(xf, g_mix, gm_w_in[j].astype(BF16), gm_b_in[j].reshape(1, -1),
                            gm_v_gain[j].reshape(1, -1), gm_w_s[j], bs_full, 256)
            w_proj = gm_w_out[j]
        else:
            inner = ssm_w_out.shape[1]
            heads = ssm_dt_bias.shape[1]
            conv_dim = ssm_conv_w.shape[2]
            w_in = ssm_w_in[j]
            zxbc = _norm_matmul(xf, g_mix, w_in[:, :inner + conv_dim].astype(BF16), BF16, 1024, 1024)
            w_dt = jnp.pad(w_in[:, inner + conv_dim:], ((0, 0), (0, LANES - heads))).astype(BF16)
            dt_raw = _norm_matmul(xf, g_mix, w_dt, F32, 1024, LANES)
            pad_h = (0, LANES - heads)
            m = _ssd(zxbc, dt_raw, ssm_conv_w[j], ssm_conv_b[j].reshape(1, -1),
                     jnp.pad(ssm_dt_bias[j], pad_h).reshape(1, LANES),
                     jnp.pad(ssm_a_log[j], pad_h).reshape(1, LANES),
                     jnp.repeat(ssm_d[j], SSM_HEAD_DIM).reshape(1, inner),
                     ssm_norm_gain[j].reshape(1, inner), batch, seq, inner)
            w_proj = ssm_w_out[j]
        xf = _tail(m, w_proj.astype(BF16), xf, ffn_norm[i].reshape(1, d),
                   ffn_w_gu[i].astype(BF16), ffn_w_down[i].astype(BF16), tail_tm, tail_th)
    return xf.reshape(batch, seq, d)
```

```python
import functools
import math

import jax
import jax.numpy as jnp
from jax import lax
from jax.experimental import pallas as pl
from jax.experimental.pallas import tpu as pltpu

F32 = jnp.float32
BF16 = jnp.bfloat16
EPS = 1e-6
LOG2E = 1.4426950408889634

LANES = 128
V7X_VMEM_BYTES = 64 * 1024 * 1024
VMEM_LIMIT = 56 * 1024 * 1024

HEAD_DIM = 64
CHUNK = 128
SSM_HEAD_DIM = 64
SSM_STATE = 128
SSM_GROUPS = 8
SSM_CONV = 4
GM_GROUPS = 16
SB_DEAD_LOG = -110.0


def _cparams(sem):
    return pltpu.CompilerParams(dimension_semantics=sem, vmem_limit_bytes=VMEM_LIMIT)


def _resident(shape):
    zeros = (0,) * len(shape)
    return pl.BlockSpec(shape, lambda *_: zeros, pipeline_mode=pl.Buffered(1))


def _rmsnorm(x, g):
    ms = jnp.mean(x * x, axis=-1, keepdims=True)
    return x * lax.rsqrt(ms + EPS) * g


def _split_bf16(x, terms):
    parts = []
    r = x
    for _ in range(terms):
        p = r.astype(BF16)
        parts.append(p)
        r = r - p.astype(F32)
    return parts


def _dot(a, b):
    return jnp.dot(a, b, preferred_element_type=F32)


def _dot_nt(a, b):
    return lax.dot_general(a, b, (((1,), (1,)), ((), ())), preferred_element_type=F32)


def _silu(x):
    return x * jax.nn.sigmoid(x)


def _softplus(x):
    return jnp.maximum(x, 0.0) + jnp.log1p(jnp.exp(-jnp.abs(x)))


def _norm_matmul_kernel(x_ref, g_ref, w_ref, *rest, tn):
    h = _rmsnorm(x_ref[...], g_ref[...]).astype(BF16)
    o_ref = rest[-1] if len(rest) == 1 else rest[1]
    for c in range(w_ref.shape[1] // tn):
        o_ref[:, c * tn:(c + 1) * tn] = _dot(h, w_ref[:, c * tn:(c + 1) * tn]).astype(o_ref.dtype)
    if len(rest) == 3:
        rest[2][...] = _dot(h, rest[0][...])


def _norm_matmul(x, g, w, tm, tn, w_f32=None):
    m, k = x.shape
    n = w.shape[1]
    in_specs = [pl.BlockSpec((tm, k), lambda i: (i, 0)), _resident((1, k)), _resident((k, n))]
    out_specs = [pl.BlockSpec((tm, n), lambda i: (i, 0))]
    out_shape = [jax.ShapeDtypeStruct((m, n), BF16)]
    args = [x, g, w]
    if w_f32 is not None:
        n2 = w_f32.shape[1]
        in_specs.append(_resident((k, n2)))
        out_specs.append(pl.BlockSpec((tm, n2), lambda i: (i, 0)))
        out_shape.append(jax.ShapeDtypeStruct((m, n2), F32))
        args.append(w_f32)
    out = pl.pallas_call(
        functools.partial(_norm_matmul_kernel, tn=tn),
        grid=(m // tm,),
        in_specs=in_specs,
        out_specs=out_specs,
        out_shape=out_shape,
        compiler_params=_cparams(("parallel",)),
        name="norm_matmul",
    )(*args)
    return out[0] if w_f32 is None else out


def _tail_kernel(m_ref, wp_ref, x_ref, g_ref, wgu_ref, wd_ref, o_ref, a_ref, *, hidden, th):
    x1 = x_ref[...] + _dot(m_ref[...], wp_ref[...])
    h = _rmsnorm(x1, g_ref[...]).astype(BF16)
    for c in range(hidden // th):
        gate = _dot(h, wgu_ref[:, c * th:(c + 1) * th])
        up = _dot(h, wgu_ref[:, hidden + c * th:hidden + (c + 1) * th])
        a_ref[:, c * th:(c + 1) * th] = (_silu(gate) * up).astype(BF16)
    o_ref[...] = x1 + _dot(a_ref[...], wd_ref[...])


def _tail(m, w_proj, x, g, w_gu, w_down, tm, th):
    rows, d = x.shape
    kin = m.shape[1]
    hidden = w_down.shape[0]
    return pl.pallas_call(
        functools.partial(_tail_kernel, hidden=hidden, th=th),
        grid=(rows // tm,),
        in_specs=[pl.BlockSpec((tm, kin), lambda i: (i, 0)),
                  _resident((kin, d)),
                  pl.BlockSpec((tm, d), lambda i: (i, 0)),
                  _resident((1, d)),
                  _resident((d, 2 * hidden)),
                  _resident((hidden, d))],
        out_specs=pl.BlockSpec((tm, d), lambda i: (i, 0)),
        out_shape=jax.ShapeDtypeStruct((rows, d), F32),
        scratch_shapes=[pltpu.VMEM((tm, hidden), BF16)],
        compiler_params=_cparams(("parallel",)),
        name="tail",
    )(m, w_proj, x, g, w_gu, w_down)


SB_FAST_BLOCKS = 3


def _sb_kernel(q_ref, k_ref, v_ref, qg_ref, kg_ref, o_ref, qn_ref, k2_ref, v2_ref, r_ref, acc_ref,
               lb_ref, lhs_ref, st_ref, w_ref, *, seq, nq):
    blk = CHUNK
    two = 2 * blk
    fast = SB_FAST_BLOCKS
    lane = lax.broadcasted_iota(jnp.int32, (1, LANES), 1)
    lo = lane < HEAD_DIM

    hk = lax.broadcasted_iota(jnp.int32, (two, LANES), 0)
    hn = lax.broadcasted_iota(jnp.int32, (two, LANES), 1)
    head_ones = jnp.where(((hk & (LANES - 1)) < HEAD_DIM) == (hn < HEAD_DIM), 1.0, 0.0).astype(BF16)

    q_scale = LOG2E / math.sqrt(HEAD_DIM)
    pad_blocks = fast - 1
    k2_ref[0:pad_blocks * two, :] = jnp.zeros((pad_blocks * two, LANES), BF16)
    v2_ref[0:pad_blocks * two, :] = jnp.zeros((pad_blocks * two, LANES), BF16)

    def prep_body(r, carry):
        rows = pl.ds(pl.multiple_of(r * two, two), two)
        kq = jnp.concatenate([k_ref[0, rows, :], q_ref[0, rows, :]], axis=0).astype(F32)
        hi, lo_part = _split_bf16(kq * kq, 2)
        ms = _dot(jnp.concatenate([hi, lo_part], axis=1), head_ones) * (1.0 / HEAD_DIM)
        kq = kq * lax.rsqrt(ms + EPS)
        kn = kq[:two] * kg_ref[...]
        qn_ref[rows, :] = (kq[two:] * (qg_ref[...] * q_scale)).astype(BF16)
        vv = v_ref[0, rows, :].astype(F32)
        for half in range(2):
            sl = slice(half * blk, (half + 1) * blk)
            head0 = pl.ds(pl.multiple_of((2 * r + half + pad_blocks) * two, two), blk)
            head1 = pl.ds(pl.multiple_of((2 * r + half + pad_blocks) * two + blk, blk), blk)
            k2_ref[head0, :] = jnp.where(lo, kn[sl], 0.0).astype(BF16)
            k2_ref[head1, :] = jnp.where(lo, 0.0, kn[sl]).astype(BF16)
            v2_ref[head0, :] = jnp.where(lo, vv[sl], 0.0).astype(BF16)
            v2_ref[head1, :] = jnp.where(lo, 0.0, vv[sl]).astype(BF16)
        return carry

    lax.fori_loop(0, seq // two, prep_body, 0)

    uj = lax.broadcasted_iota(jnp.int32, (two, two), 0) & (blk - 1)
    us = lax.broadcasted_iota(jnp.int32, (two, two), 1)
    suffix_total = jnp.where((uj > us) | (us >= blk), 1.0, 0.0).astype(BF16)
    key_minus_query = ((lax.broadcasted_iota(jnp.int32, (blk, two), 1) & (blk - 1))
                       - lax.broadcasted_iota(jnp.int32, (blk, two), 0))
    dead_log2 = SB_DEAD_LOG * LOG2E

    def log_sigmoids(z):
        log_beta = jnp.minimum(z, 0.0) - jnp.log(1.0 + jnp.exp2(-jnp.abs(z))) * LOG2E
        return log_beta, log_beta - z

    def hi_lo(x):
        hi, lo_part = _split_bf16(x, 2)
        return jnp.concatenate([hi, lo_part], axis=1)

    def suffix_and_total(st0, st1):
        return (jnp.concatenate([st0[:, :blk], st1[:, :blk]], axis=1),
                jnp.concatenate([st0[:, blk:], st1[:, blk:]], axis=1))

    def block(qb, kb, valid, r_in):
        krows = pl.ds(pl.multiple_of((kb + pad_blocks) * two, two), two)
        log_beta, log_1m = log_sigmoids(_dot_nt(qb, k2_ref[krows, :]))
        log_1m = jnp.where(valid, log_1m, 0.0)
        suffix, total = suffix_and_total(*[_dot(hi_lo(log_1m[:, h * blk:(h + 1) * blk]), suffix_total)
                                           for h in range(2)])
        w = jnp.where(valid, jnp.exp2(log_beta + suffix + r_in), 0.0)
        return _dot(w.astype(BF16), v2_ref[krows, :]), total

    diag_valid = key_minus_query < 0

    def fast_blocks(g, qbs, first_group):
        def key_rows(j):
            return pl.ds(pl.multiple_of((g * nq + j) * two, two), fast * two)

        def padding(j, i):
            return first_group and j - i < 0

        def scores(js):
            for j in js:
                log_beta, log_1m = log_sigmoids(_dot_nt(qbs[j], k2_ref[key_rows(j), :]))
                lb_ref[j] = log_beta
                for i in range(fast):
                    c0 = (fast - 1 - i) * two
                    l1 = log_1m[:, c0:c0 + two]
                    if padding(j, i):
                        l1 = jnp.zeros_like(l1)
                    elif i == 0:
                        l1 = jnp.where(diag_valid, l1, 0.0)
                    for h in range(2):
                        row0 = ((j * fast + i) * 2 + h) * blk
                        lhs_ref[row0:row0 + blk, :] = hi_lo(l1[:, h * blk:(h + 1) * blk])

        def suffix_sums(js):
            rows = slice(js[0] * fast * two, (js[-1] + 1) * fast * two)
            st_ref[rows, :] = _dot(lhs_ref[rows, :], suffix_total)

        def weights_and_values(js):
            for j in js:
                r_run = None
                for i in range(fast):
                    c0 = (fast - 1 - i) * two
                    row0 = (j * fast + i) * two
                    suffix, total = suffix_and_total(st_ref[row0:row0 + blk, :], st_ref[row0 + blk:row0 + two, :])
                    arg = lb_ref[j, :, c0:c0 + two] + suffix
                    if r_run is not None:
                        arg = arg + r_run
                    w = jnp.exp2(arg)
                    if padding(j, i):
                        w = jnp.zeros_like(w)
                    elif i == 0:
                        w = jnp.where(diag_valid, w, 0.0)
                    w_ref[j, :, c0:c0 + two] = w.astype(BF16)
                    r_run = total if r_run is None else r_run + total
                r_ref[j] = r_run
                acc_ref[j] = _dot(w_ref[j], v2_ref[key_rows(j), :])

        half_a, half_b = list(range(nq // 2)), list(range(nq // 2, nq))
        scores(half_a)
        suffix_sums(half_a)
        scores(half_b)
        suffix_sums(half_b)
        weights_and_values(half_a)
        weights_and_values(half_b)

    def group_body(g, carry):
        qbs = []
        for j in range(nq):
            rows = pl.ds(pl.multiple_of((g * nq + j) * blk, blk), blk)
            qbs.append(qn_ref[rows, :])

        @pl.when(g == 0)
        def _():
            fast_blocks(0, qbs, first_group=True)

        @pl.when(g > 0)
        def _():
            fast_blocks(g, qbs, first_group=False)

        def max_r(i_next):
            m = None
            for j in range(nq):
                r_j = jnp.where(g * nq + j >= i_next, r_ref[j], -jnp.inf)
                m = r_j if m is None else jnp.maximum(m, r_j)
            return jnp.max(m)

        last_block = g * nq + nq - 1

        def cond(c):
            i, m = c
            return (i <= last_block) & (m > dead_log2)

        def body(c):
            i, _ = c
            for j in range(nq):
                kb = g * nq + j - i
                offs = jnp.where(i > 0, blk, 0) - jnp.where(kb < 0, two, 0)
                pv, tot = block(qbs[j], jnp.maximum(kb, 0), key_minus_query < offs, r_ref[j])
                acc_ref[j] += pv
                r_ref[j] += tot
            return i + 1, max_r(i + 1)

        lax.while_loop(cond, body, (jnp.int32(fast), max_r(fast)))
        for j in range(nq):
            rows = pl.ds(pl.multiple_of((g * nq + j) * blk, blk), blk)
            o_ref[0, rows, :] = acc_ref[j].astype(o_ref.dtype)
        return carry

    lax.fori_loop(0, seq // (blk * nq), group_body, 0)


def _sb_attention(qkv, q_gain, k_gain, nq=8):
    assert nq >= SB_FAST_BLOCKS - 1
    pad_rows = (SB_FAST_BLOCKS - 1) * 2 * CHUNK
    b, seq, three_d = qkv.shape
    d = three_d // 3
    pairs = d // LANES
    qg = jnp.tile(q_gain.astype(F32), 2).reshape(1, LANES)
    kg = jnp.tile(k_gain.astype(F32), 2).reshape(1, LANES)
    blk = CHUNK
    return pl.pallas_call(
        functools.partial(_sb_kernel, seq=seq, nq=nq),
        grid=(b, pairs),
        in_specs=[pl.BlockSpec((1, seq, LANES), lambda i, p: (i, 0, p)),
                  pl.BlockSpec((1, seq, LANES), lambda i, p: (i, 0, pairs + p)),
                  pl.BlockSpec((1, seq, LANES), lambda i, p: (i, 0, 2 * pairs + p)),
                  pl.BlockSpec((1, LANES), lambda i, p: (0, 0)),
                  pl.BlockSpec((1, LANES), lambda i, p: (0, 0))],
        out_specs=pl.BlockSpec((1, seq, LANES), lambda i, p: (i, 0, p)),
        out_shape=jax.ShapeDtypeStruct((b, seq, d), BF16),
        scratch_shapes=[pltpu.VMEM((seq, LANES), BF16),
                        pltpu.VMEM((2 * seq + pad_rows, LANES), BF16),
                        pltpu.VMEM((2 * seq + pad_rows, LANES), BF16),
                        pltpu.VMEM((nq, blk, 2 * blk), F32),
                        pltpu.VMEM((nq, blk, LANES), F32),
                        pltpu.VMEM((nq, blk, SB_FAST_BLOCKS * 2 * blk), F32),
                        pltpu.VMEM((nq * SB_FAST_BLOCKS * 2 * blk, 2 * blk), BF16),
                        pltpu.VMEM((nq * SB_FAST_BLOCKS * 2 * blk, 2 * blk), F32),
                        pltpu.VMEM((nq, blk, SB_FAST_BLOCKS * 2 * blk), BF16)],

        compiler_params=_cparams(("parallel", "parallel")),
        name="sb_attention",
    )(qkv, qkv, qkv, qg, kg)


def _gelu(x):
    return 0.5 * x * (1.0 + lax.erf(x * (1.0 / math.sqrt(2.0))))


def _gmlp_kernel(x_ref, g_ref, w_ref, b_ref, vg_ref, ws_ref, bs_ref, o_ref, u_ref, vn_ref, *, half):
    tm = x_ref.shape[0]
    h = _rmsnorm(x_ref[...], g_ref[...]).astype(BF16)
    u_ref[...] = _gelu(_dot(h, w_ref[:, :half]) + b_ref[:, :half])
    v = _gelu(_dot(h, w_ref[:, half:]) + b_ref[:, half:])
    vn_ref[...] = _rmsnorm(v, vg_ref[...]).astype(BF16)
    t_idx = lax.broadcasted_iota(jnp.int32, (CHUNK, CHUNK), 0)
    s_idx = lax.broadcasted_iota(jnp.int32, (CHUNK, CHUNK), 1)
    causal = t_idx >= s_idx
    for grp in range(half // LANES):
        cols = slice(grp * LANES, (grp + 1) * LANES)
        w_s = jnp.where(causal, ws_ref[grp], 0.0).astype(BF16)
        for c in range(tm // CHUNK):
            rows = slice(c * CHUNK, (c + 1) * CHUNK)
            mixed = _dot(w_s, vn_ref[rows, cols]) + bs_ref[:, cols]
            o_ref[rows, cols] = (u_ref[rows, cols] * mixed).astype(o_ref.dtype)


def _gmlp_front(x, g, w_in, b_in, v_gain, w_s, bs_full, tm):
    rows, d = x.shape
    half = w_in.shape[1] // 2
    groups = w_s.shape[0]
    return pl.pallas_call(
        functools.partial(_gmlp_kernel, half=half),
        grid=(rows // tm,),
        in_specs=[pl.BlockSpec((tm, d), lambda i: (i, 0)),
                  _resident((1, d)),
                  _resident((d, 2 * half)),
                  _resident((1, 2 * half)),
                  _resident((1, half)),
                  _resident((groups, CHUNK, CHUNK)),
                  _resident((CHUNK, half))],
        out_specs=pl.BlockSpec((tm, half), lambda i: (i, 0)),
        out_shape=jax.ShapeDtypeStruct((rows, half), BF16),
        scratch_shapes=[pltpu.VMEM((tm, half), F32), pltpu.VMEM((tm, half), BF16)],
        compiler_params=_cparams(("parallel",)),
        name="gmlp_front",
    )(x, g, w_in, b_in, v_gain, w_s, bs_full)


def _ssd_kernel(z_ref, xs_ref, bc_ref, dt_ref, cw_ref, cbias_ref, dtb_ref, alog_ref, dexp_ref, ng_ref, o_ref,
                raw_ref, expand_ref, state_ref, y_ref, *, inner):
    L = CHUNK
    gstate = SSM_GROUPS * SSM_STATE
    conv_dim = inner + 2 * gstate
    gw = inner // SSM_GROUPS
    hpg = gw // SSM_HEAD_DIM

    @pl.when(pl.program_id(1) == 0)
    def _():
        raw_ref[L:2 * L, :] = jnp.zeros((L, conv_dim), BF16)
        state_ref[...] = jnp.zeros(state_ref.shape, F32)
        ek = lax.broadcasted_iota(jnp.int32, (2 * LANES, inner), 0) & (LANES - 1)
        ec = lax.broadcasted_iota(jnp.int32, (2 * LANES, inner), 1) // SSM_HEAD_DIM
        expand_ref[...] = jnp.where(ek == ec, 1.0, 0.0).astype(BF16)

    raw_ref[0:L, :] = raw_ref[L:2 * L, :]
    raw_ref[L:2 * L, 0:inner] = xs_ref[...]
    raw_ref[L:2 * L, inner:conv_dim] = bc_ref[...]

    tok = lax.broadcasted_iota(jnp.int32, (L, 2 * L), 0)
    src = lax.broadcasted_iota(jnp.int32, (L, 2 * L), 1)

    def conv_silu(cols):
        raw = raw_ref[:, cols]
        acc = cbias_ref[:, cols] + cw_ref[SSM_CONV - 1:SSM_CONV, cols] * raw[L:2 * L].astype(F32)
        for k in range(SSM_CONV - 1):
            shift = jnp.where(src == tok + (L - (SSM_CONV - 1) + k), 1.0, 0.0).astype(BF16)
            acc = acc + cw_ref[k:k + 1, cols] * _dot(shift, raw)
        return _silu(acc)

    xs = conv_silu(slice(0, inner))
    bcm = conv_silu(slice(inner, conv_dim))

    dt = _softplus(dt_ref[...] + dtb_ref[...])
    a = dt * (-jnp.exp(alog_ref[...]))
    t_idx = lax.broadcasted_iota(jnp.int32, (L, L), 0)
    s_idx = lax.broadcasted_iota(jnp.int32, (L, L), 1)
    causal = t_idx >= s_idx
    tril = jnp.where(causal, 1.0, 0.0).astype(BF16)
    cum3 = _dot(tril, jnp.concatenate(_split_bf16(a, 3), axis=1))
    a_cum = cum3[:, :LANES] + cum3[:, LANES:2 * LANES] + cum3[:, 2 * LANES:]
    a_last = a_cum[L - 1:L, :]
    wgt = dt * jnp.exp(a_last - a_cum)
    chunk_decay = jnp.broadcast_to(jnp.exp(a_last), (16, LANES))
    hi, lo_part = _split_bf16(jnp.concatenate([wgt, chunk_decay], axis=0), 2)
    expanded = _dot(jnp.concatenate([hi, lo_part], axis=1), expand_ref[...])
    xw = (xs * expanded[:L]).astype(BF16)
    cd_exp = expanded[L:L + 1]
    a_cum_t = a_cum.T
    dt_t = dt.T
    exp_a_cum = jnp.exp(a_cum)

    head_of_lane = lax.broadcasted_iota(jnp.int32, (1, gw), 1) // SSM_HEAD_DIM

    def per_head_rows(x):
        return jnp.concatenate([jnp.where(head_of_lane == hh, x, 0.0).astype(BF16) for hh in range(hpg)], axis=0)

    for grp in range(SSM_GROUPS):
        b_g = bcm[:, grp * SSM_STATE:(grp + 1) * SSM_STATE]
        c_g = bcm[:, gstate + grp * SSM_STATE:gstate + (grp + 1) * SSM_STATE]
        cols = slice(grp * gw, (grp + 1) * gw)
        cb = _dot_nt(c_g.astype(BF16), b_g.astype(BF16))
        prev = state_ref[grp]
        m_parts, e_parts = [], []
        for hh in range(hpg):
            head = grp * hpg + hh
            a_col = jnp.broadcast_to(a_cum[:, head:head + 1], (L, L))
            a_row = jnp.broadcast_to(a_cum_t[head:head + 1, :], (L, L))
            decay = jnp.exp(jnp.where(causal, a_col - a_row, -jnp.inf))
            dt_row = jnp.broadcast_to(dt_t[head:head + 1, :], (L, L))
            m_parts.append((cb * decay * dt_row).astype(BF16))
            e_parts.append((c_g * jnp.broadcast_to(exp_a_cum[:, head:head + 1], (L, SSM_STATE))).astype(BF16))
        lhs = jnp.concatenate(m_parts + e_parts, axis=1)
        rhs = jnp.concatenate([per_head_rows(xs[:, cols]), per_head_rows(prev)], axis=0)
        y_ref[:, cols] = _dot(lhs, rhs)
        state_ref[grp] = prev * cd_exp[:, cols] + _dot(b_g.T.astype(BF16), xw[:, cols])

    z = z_ref[...].astype(F32)
    yg = (y_ref[...] + xs * dexp_ref[...]) * _silu(z)
    for grp in range(SSM_GROUPS):
        cols = slice(grp * gw, (grp + 1) * gw)
        o_ref[:, cols] = _rmsnorm(yg[:, cols], ng_ref[:, cols]).astype(o_ref.dtype)


def _ssd(zxbc, dt_raw, conv_w, conv_b, dt_bias, a_log, d_exp, norm_gain, batch, seq, inner):
    gstate = SSM_GROUPS * SSM_STATE
    conv_dim = inner + 2 * gstate
    nc = seq // CHUNK

    def row_block(col):
        return pl.BlockSpec((CHUNK, inner), lambda b, c: (b * nc + c, col))

    return pl.pallas_call(
        functools.partial(_ssd_kernel, inner=inner),
        grid=(batch, nc),
        in_specs=[row_block(0), row_block(1), row_block(2),
                  pl.BlockSpec((CHUNK, LANES), lambda b, c: (b * nc + c, 0)),
                  _resident((SSM_CONV, conv_dim)), _resident((1, conv_dim)),
                  _resident((1, LANES)), _resident((1, LANES)),
                  _resident((1, inner)), _resident((1, inner))],
        out_specs=pl.BlockSpec((CHUNK, inner), lambda b, c: (b * nc + c, 0)),
        out_shape=jax.ShapeDtypeStruct((batch * seq, inner), BF16),
        scratch_shapes=[pltpu.VMEM((2 * CHUNK, conv_dim), BF16),
                        pltpu.VMEM((2 * LANES, inner), BF16),
                        pltpu.VMEM((SSM_GROUPS, SSM_STATE, inner // SSM_GROUPS), F32),
                        pltpu.VMEM((CHUNK, inner), F32)],
        compiler_params=_cparams(("parallel", "arbitrary")),
        name="ssd",
    )(zxbc, zxbc, zxbc, dt_raw, conv_w, conv_b, dt_bias, a_log, d_exp, norm_gain)


def kernel(x, mix_norm, ffn_norm, sb_w_qkv, sb_q_gain, sb_k_gain, sb_w_o, gm_w_in, gm_b_in, gm_v_gain,
           gm_w_s, gm_b_s, gm_w_out, ssm_w_in, ssm_conv_w, ssm_conv_b, ssm_dt_bias, ssm_a_log, ssm_d,
           ssm_norm_gain, ssm_w_out, ffn_w_gu, ffn_w_down):
    batch, seq, d = x.shape
    rows = batch * seq
    depth = mix_norm.shape[0]
    xf = x.reshape(rows, d)
    tail_tm, tail_th = 512, 256
    proj_tm, proj_tn = 512, 512
    for i in range(depth):
        kind, j = i % 3, i // 3
        g_mix = mix_norm[i].reshape(1, d)
        if kind == 0:
            qkv = _norm_matmul(xf, g_mix, sb_w_qkv[j].astype(BF16), proj_tm, proj_tn)
            m = _sb_attention(qkv.reshape(batch, seq, -1), sb_q_gain[j], sb_k_gain[j]).reshape(rows, -1)
            w_proj = sb_w_o[j]
        elif kind == 1:
            half = gm_w_in.shape[2] // 2
            bs_full = jnp.repeat(gm_b_s[j].T, half // GM_GROUPS, axis=1)
            m = _gmlp_front(xf, g_mix, gm_w_in[j].astype(BF16), gm_b_in[j].reshape(1, -1),
                            gm_v_gain[j].reshape(1, -1), gm_w_s[j], bs_full, 256)
            w_proj = gm_w_out[j]
        else:
            inner = ssm_w_out.shape[1]
            heads = ssm_dt_bias.shape[1]
            conv_dim = ssm_conv_w.shape[2]
            w_in = ssm_w_in[j]
            w_dt = jnp.pad(w_in[:, inner + conv_dim:], ((0, 0), (0, LANES - heads))).astype(BF16)
            zxbc, dt_raw = _norm_matmul(xf, g_mix, w_in[:, :inner + conv_dim].astype(BF16), proj_tm, proj_tn,
                                        w_f32=w_dt)
            pad_h = (0, LANES - heads)
            m = _ssd(zxbc, dt_raw, ssm_conv_w[j], ssm_conv_b[j].reshape(1, -1),
                     jnp.pad(ssm_dt_bias[j], pad_h).reshape(1, LANES),
                     jnp.pad(ssm_a_log[j], pad_h).reshape(1, LANES),
                     jnp.repeat(ssm_d[j], SSM_HEAD_DIM).reshape(1, inner),
                     ssm_norm_gain[j].reshape(1, inner), batch, seq, inner)
            w_proj = ssm_w_out[j]
        xf = _tail(m, w_proj.astype(BF16), xf, ffn_norm[i].reshape(1, d),
                   ffn_w_gu[i].astype(BF16), ffn_w_down[i].astype(BF16), tail_tm, tail_th)
    return xf.reshape(batch, seq, d)
```

```python
import functools
import math

import jax
import jax.numpy as jnp
from jax import lax
from jax.experimental import pallas as pl
from jax.experimental.pallas import tpu as pltpu

F32 = jnp.float32
BF16 = jnp.bfloat16
EPS = 1e-6
LOG2E = 1.4426950408889634

LANES = 128
V7X_VMEM_BYTES = 64 * 1024 * 1024
VMEM_LIMIT = 56 * 1024 * 1024

HEAD_DIM = 64
CHUNK = 128
SSM_HEAD_DIM = 64
SSM_STATE = 128
SSM_GROUPS = 8
SSM_CONV = 4
GM_GROUPS = 16
SB_DEAD_LOG = -110.0


def _cparams(sem):
    return pltpu.CompilerParams(dimension_semantics=sem, vmem_limit_bytes=VMEM_LIMIT)


def _resident(shape):
    zeros = (0,) * len(shape)
    return pl.BlockSpec(shape, lambda *_: zeros, pipeline_mode=pl.Buffered(1))


def _rmsnorm(x, g):
    ms = jnp.mean(x * x, axis=-1, keepdims=True)
    return x * lax.rsqrt(ms + EPS) * g


def _split_bf16(x, terms):
    parts = []
    r = x
    for _ in range(terms):
        p = r.astype(BF16)
        parts.append(p)
        r = r - p.astype(F32)
    return parts


def _dot(a, b):
    return jnp.dot(a, b, preferred_element_type=F32)


def _dot_nt(a, b):
    return lax.dot_general(a, b, (((1,), (1,)), ((), ())), preferred_element_type=F32)


def _silu(x):
    return x * jax.nn.sigmoid(x)


def _softplus(x):
    return jnp.maximum(x, 0.0) + jnp.log1p(jnp.exp(-jnp.abs(x)))


def _norm_matmul_kernel(x_ref, g_ref, w_ref, *rest, tn):
    h = _rmsnorm(x_ref[...], g_ref[...]).astype(BF16)
    o_ref = rest[-1] if len(rest) == 1 else rest[1]
    for c in range(w_ref.shape[1] // tn):
        o_ref[:, c * tn:(c + 1) * tn] = _dot(h, w_ref[:, c * tn:(c + 1) * tn]).astype(o_ref.dtype)
    if len(rest) == 3:
        rest[2][...] = _dot(h, rest[0][...])


def _norm_matmul(x, g, w, tm, tn, w_f32=None):
    m, k = x.shape
    n = w.shape[1]
    in_specs = [pl.BlockSpec((tm, k), lambda i: (i, 0)), _resident((1, k)), _resident((k, n))]
    out_specs = [pl.BlockSpec((tm, n), lambda i: (i, 0))]
    out_shape = [jax.ShapeDtypeStruct((m, n), BF16)]
    args = [x, g, w]
    if w_f32 is not None:
        n2 = w_f32.shape[1]
        in_specs.append(_resident((k, n2)))
        out_specs.append(pl.BlockSpec((tm, n2), lambda i: (i, 0)))
        out_shape.append(jax.ShapeDtypeStruct((m, n2), F32))
        args.append(w_f32)
    out = pl.pallas_call(
        functools.partial(_norm_matmul_kernel, tn=tn),
        grid=(m // tm,),
        in_specs=in_specs,
        out_specs=out_specs,
        out_shape=out_shape,
        compiler_params=_cparams(("parallel",)),
        name="norm_matmul",
    )(*args)
    return out[0] if w_f32 is None else out


def _tail_kernel(m_ref, wp_ref, x_ref, g_ref, wgu_ref, wd_ref, o_ref, a_ref, *, hidden, th):
    x1 = x_ref[...] + _dot(m_ref[...], wp_ref[...])
    h = _rmsnorm(x1, g_ref[...]).astype(BF16)
    for c in range(hidden // th):
        gate = _dot(h, wgu_ref[:, c * th:(c + 1) * th])
        up = _dot(h, wgu_ref[:, hidden + c * th:hidden + (c + 1) * th])
        a_ref[:, c * th:(c + 1) * th] = (_silu(gate) * up).astype(BF16)
    o_ref[...] = x1 + _dot(a_ref[...], wd_ref[...])


def _tail(m, w_proj, x, g, w_gu, w_down, tm, th):
    rows, d = x.shape
    kin = m.shape[1]
    hidden = w_down.shape[0]
    return pl.pallas_call(
        functools.partial(_tail_kernel, hidden=hidden, th=th),
        grid=(rows // tm,),
        in_specs=[pl.BlockSpec((tm, kin), lambda i: (i, 0)),
                  _resident((kin, d)),
                  pl.BlockSpec((tm, d), lambda i: (i, 0)),
                  _resident((1, d)),
                  _resident((d, 2 * hidden)),
                  _resident((hidden, d))],
        out_specs=pl.BlockSpec((tm, d), lambda i: (i, 0)),
        out_shape=jax.ShapeDtypeStruct((rows, d), F32),
        scratch_shapes=[pltpu.VMEM((tm, hidden), BF16)],
        compiler_params=_cparams(("parallel",)),
        name="tail",
    )(m, w_proj, x, g, w_gu, w_down)


SB_TILE = 64
SB_FAST_BLOCKS = 4


def _sb_kernel(q_ref, k_ref, v_ref, qg_ref, kg_ref, o_ref, qn_ref, k2_ref, v2_ref, r_ref, rmax_ref, acc_ref,
               lb_ref, lhs_ref, st_ref, w_ref, *, seq, nt):
    tb = SB_TILE
    fast = SB_FAST_BLOCKS
    prep_rows = 2 * LANES
    lane = lax.broadcasted_iota(jnp.int32, (1, LANES), 1)
    lo = lane < HEAD_DIM

    hk = lax.broadcasted_iota(jnp.int32, (2 * LANES, LANES), 0)
    hn = lax.broadcasted_iota(jnp.int32, (2 * LANES, LANES), 1)
    head_ones = jnp.where(((hk & (LANES - 1)) < HEAD_DIM) == (hn < HEAD_DIM), 1.0, 0.0).astype(BF16)

    q_scale = LOG2E / math.sqrt(HEAD_DIM)
    pad_blocks = fast - 1
    k2_ref[0:pad_blocks * LANES, :] = jnp.zeros((pad_blocks * LANES, LANES), BF16)
    v2_ref[0:pad_blocks * LANES, :] = jnp.zeros((pad_blocks * LANES, LANES), BF16)

    def prep_body(r, carry):
        rows = pl.ds(pl.multiple_of(r * prep_rows, prep_rows), prep_rows)
        kq = jnp.concatenate([k_ref[0, rows, :], q_ref[0, rows, :]], axis=0).astype(F32)
        hi, lo_part = _split_bf16(kq * kq, 2)
        ms = _dot(jnp.concatenate([hi, lo_part], axis=1), head_ones) * (1.0 / HEAD_DIM)
        kq = kq * lax.rsqrt(ms + EPS)
        kn = kq[:prep_rows] * kg_ref[...]
        qn_ref[rows, :] = (kq[prep_rows:] * (qg_ref[...] * q_scale)).astype(BF16)
        vv = v_ref[0, rows, :].astype(F32)
        for part in range(prep_rows // tb):
            sl = slice(part * tb, (part + 1) * tb)
            base = (r * (prep_rows // tb) + part + pad_blocks) * LANES
            head0 = pl.ds(pl.multiple_of(base, LANES), tb)
            head1 = pl.ds(pl.multiple_of(base + tb, tb), tb)
            k2_ref[head0, :] = jnp.where(lo, kn[sl], 0.0).astype(BF16)
            k2_ref[head1, :] = jnp.where(lo, 0.0, kn[sl]).astype(BF16)
            v2_ref[head0, :] = jnp.where(lo, vv[sl], 0.0).astype(BF16)
            v2_ref[head1, :] = jnp.where(lo, 0.0, vv[sl]).astype(BF16)
        return carry

    lax.fori_loop(0, seq // prep_rows, prep_body, 0, unroll=4)

    uj = lax.broadcasted_iota(jnp.int32, (2 * LANES, 2 * LANES), 0) & (LANES - 1)
    uc = lax.broadcasted_iota(jnp.int32, (2 * LANES, 2 * LANES), 1)
    same_head = (uj >= tb) == ((uc & (LANES - 1)) >= tb)
    suffix_total = jnp.where(same_head & ((uc >= LANES) | ((uj & (tb - 1)) > (uc & (tb - 1)))),
                             1.0, 0.0).astype(BF16)
    key_minus_query = ((lax.broadcasted_iota(jnp.int32, (tb, LANES), 1) & (tb - 1))
                       - lax.broadcasted_iota(jnp.int32, (tb, LANES), 0))
    diag_valid = key_minus_query < 0
    dead_log2 = SB_DEAD_LOG * LOG2E

    def log_sigmoids(z):
        log_beta = jnp.minimum(z, 0.0) - jnp.log(1.0 + jnp.exp2(-jnp.abs(z))) * LOG2E
        return log_beta, log_beta - z

    def hi_lo(x):
        hi, lo_part = _split_bf16(x, 2)
        return jnp.concatenate([hi, lo_part], axis=1)

    def set_r(j, r):
        r_ref[j] = r
        rmax_ref[j] = jnp.max(r.reshape(tb // 8, 8, LANES), axis=0)

    def block(qt, kb, valid, r_in):
        krows = pl.ds(pl.multiple_of((kb + pad_blocks) * LANES, LANES), LANES)
        log_beta, log_1m = log_sigmoids(_dot_nt(qt, k2_ref[krows, :]))
        st = _dot(hi_lo(jnp.where(valid, log_1m, 0.0)), suffix_total)
        w = jnp.where(valid, jnp.exp2(log_beta + st[:, :LANES] + r_in), 0.0)
        return _dot(w.astype(BF16), v2_ref[krows, :]), st[:, LANES:]

    def fast_blocks(g, qts, first_group):
        def key_rows(j):
            return pl.ds(pl.multiple_of((g * nt + j) * LANES, LANES), fast * LANES)

        def padding(j, i):
            return first_group and j - i < 0

        def scores(js):
            for j in js:
                log_beta, log_1m = log_sigmoids(_dot_nt(qts[j], k2_ref[key_rows(j), :]))
                lb_ref[j] = log_beta
                for i in range(fast):
                    c0 = (fast - 1 - i) * LANES
                    l1 = log_1m[:, c0:c0 + LANES]
                    if padding(j, i):
                        l1 = jnp.zeros_like(l1)
                    elif i == 0:
                        l1 = jnp.where(diag_valid, l1, 0.0)
                    row0 = (j * fast + i) * tb
                    lhs_ref[row0:row0 + tb, :] = hi_lo(l1)

        def suffix_sums(js):
            rows = slice(js[0] * fast * tb, (js[-1] + 1) * fast * tb)
            st_ref[rows, :] = _dot(lhs_ref[rows, :], suffix_total)

        def weights_and_values(js):
            for j in js:
                r_run = None
                for i in range(fast):
                    c0 = (fast - 1 - i) * LANES
                    row0 = (j * fast + i) * tb
                    st = st_ref[row0:row0 + tb, :]
                    arg = lb_ref[j, :, c0:c0 + LANES] + st[:, :LANES]
                    if r_run is not None:
                        arg = arg + r_run
                    w = jnp.exp2(arg)
                    if padding(j, i):
                        w = jnp.zeros_like(w)
                    elif i == 0:
                        w = jnp.where(diag_valid, w, 0.0)
                    w_ref[j, :, c0:c0 + LANES] = w.astype(BF16)
                    r_run = st[:, LANES:] if r_run is None else r_run + st[:, LANES:]
                set_r(j, r_run)
                acc_ref[j] = _dot(w_ref[j], v2_ref[key_rows(j), :])

        half_a, half_b = list(range(nt // 2)), list(range(nt // 2, nt))
        scores(half_a)
        suffix_sums(half_a)
        scores(half_b)
        suffix_sums(half_b)
        weights_and_values(half_a)
        weights_and_values(half_b)

    def group_body(g, carry):
        def tile_rows(j):
            return pl.ds(pl.multiple_of((g * nt + j) * tb, tb), tb)

        qts = [qn_ref[tile_rows(j), :] for j in range(nt)]

        @pl.when(g == 0)
        def _():
            fast_blocks(0, qts, first_group=True)

        @pl.when(g > 0)
        def _():
            fast_blocks(g, qts, first_group=False)

        def max_r(i_next):
            m = None
            for j in range(nt):
                r_j = jnp.where(g * nt + j >= i_next, rmax_ref[j], -jnp.inf)
                m = r_j if m is None else jnp.maximum(m, r_j)
            return jnp.max(m)

        def cond(c):
            _, m = c
            return m > dead_log2

        def body(c):
            i, _ = c
            for j in range(nt):
                kb = g * nt + j - i
                offs = jnp.where(i > 0, tb, 0) - jnp.where(kb < 0, 2 * tb, 0)
                pv, tot = block(qts[j], jnp.maximum(kb, 0), key_minus_query < offs, r_ref[j])
                acc_ref[j] += pv
                set_r(j, r_ref[j] + tot)
            return i + 1, max_r(i + 1)

        lax.while_loop(cond, body, (jnp.int32(fast), max_r(fast)))
        for j in range(nt):
            o_ref[0, tile_rows(j), :] = acc_ref[j].astype(o_ref.dtype)
        return carry

    lax.fori_loop(0, seq // (tb * nt), group_body, 0)


def _sb_attention(qkv, q_gain, k_gain, nt=16):
    assert nt >= SB_FAST_BLOCKS - 1
    assert 2 * SB_TILE == LANES and HEAD_DIM == SB_TILE
    b, seq, three_d = qkv.shape
    d = three_d // 3
    pairs = d // LANES
    qg = jnp.tile(q_gain.astype(F32), 2).reshape(1, LANES)
    kg = jnp.tile(k_gain.astype(F32), 2).reshape(1, LANES)
    tb, fast = SB_TILE, SB_FAST_BLOCKS
    key_rows = (seq // tb + fast - 1) * LANES
    return pl.pallas_call(
        functools.partial(_sb_kernel, seq=seq, nt=nt),
        grid=(b, pairs),
        in_specs=[pl.BlockSpec((1, seq, LANES), lambda i, p: (i, 0, p)),
                  pl.BlockSpec((1, seq, LANES), lambda i, p: (i, 0, pairs + p)),
                  pl.BlockSpec((1, seq, LANES), lambda i, p: (i, 0, 2 * pairs + p)),
                  pl.BlockSpec((1, LANES), lambda i, p: (0, 0)),
                  pl.BlockSpec((1, LANES), lambda i, p: (0, 0))],
        out_specs=pl.BlockSpec((1, seq, LANES), lambda i, p: (i, 0, p)),
        out_shape=jax.ShapeDtypeStruct((b, seq, d), BF16),
        scratch_shapes=[pltpu.VMEM((seq, LANES), BF16),
                        pltpu.VMEM((key_rows, LANES), BF16),
                        pltpu.VMEM((key_rows, LANES), BF16),
                        pltpu.VMEM((nt, tb, LANES), F32),
                        pltpu.VMEM((nt, 8, LANES), F32),
                        pltpu.VMEM((nt, tb, LANES), F32),
                        pltpu.VMEM((nt, tb, fast * LANES), F32),
                        pltpu.VMEM((nt * fast * tb, 2 * LANES), BF16),
                        pltpu.VMEM((nt * fast * tb, 2 * LANES), F32),
                        pltpu.VMEM((nt, tb, fast * LANES), BF16)],
        compiler_params=_cparams(("parallel", "parallel")),
        name="sb_attention",
    )(qkv, qkv, qkv, qg, kg)


def _gelu(x):
    return 0.5 * x * (1.0 + lax.erf(x * (1.0 / math.sqrt(2.0))))


def _gmlp_kernel(x_ref, g_ref, w_ref, b_ref, vg_ref, ws_ref, bs_ref, o_ref, u_ref, vn_ref, *, half):
    tm = x_ref.shape[0]
    h = _rmsnorm(x_ref[...], g_ref[...]).astype(BF16)
    u_ref[...] = _gelu(_dot(h, w_ref[:, :half]) + b_ref[:, :half])
    v = _gelu(_dot(h, w_ref[:, half:]) + b_ref[:, half:])
    vn_ref[...] = _rmsnorm(v, vg_ref[...]).astype(BF16)
    t_idx = lax.broadcasted_iota(jnp.int32, (CHUNK, CHUNK), 0)
    s_idx = lax.broadcasted_iota(jnp.int32, (CHUNK, CHUNK), 1)
    causal = t_idx >= s_idx
    for grp in range(half // LANES):
        cols = slice(grp * LANES, (grp + 1) * LANES)
        w_s = jnp.where(causal, ws_ref[grp], 0.0).astype(BF16)
        for c in range(tm // CHUNK):
            rows = slice(c * CHUNK, (c + 1) * CHUNK)
            mixed = _dot(w_s, vn_ref[rows, cols]) + bs_ref[:, cols]
            o_ref[rows, cols] = (u_ref[rows, cols] * mixed).astype(o_ref.dtype)


def _gmlp_front(x, g, w_in, b_in, v_gain, w_s, bs_full, tm):
    rows, d = x.shape
    half = w_in.shape[1] // 2
    groups = w_s.shape[0]
    return pl.pallas_call(
        functools.partial(_gmlp_kernel, half=half),
        grid=(rows // tm,),
        in_specs=[pl.BlockSpec((tm, d), lambda i: (i, 0)),
                  _resident((1, d)),
                  _resident((d, 2 * half)),
                  _resident((1, 2 * half)),
                  _resident((1, half)),
                  _resident((groups, CHUNK, CHUNK)),
                  _resident((CHUNK, half))],
        out_specs=pl.BlockSpec((tm, half), lambda i: (i, 0)),
        out_shape=jax.ShapeDtypeStruct((rows, half), BF16),
        scratch_shapes=[pltpu.VMEM((tm, half), F32), pltpu.VMEM((tm, half), BF16)],
        compiler_params=_cparams(("parallel",)),
        name="gmlp_front",
    )(x, g, w_in, b_in, v_gain, w_s, bs_full)


def _ssd_kernel(z_ref, xs_ref, bc_ref, dt_ref, cw_ref, cbias_ref, dtb_ref, alog_ref, dexp_ref, ng_ref, o_ref,
                raw_ref, expand_ref, state_ref, y_ref, *, inner):
    L = CHUNK
    gstate = SSM_GROUPS * SSM_STATE
    conv_dim = inner + 2 * gstate
    gw = inner // SSM_GROUPS
    hpg = gw // SSM_HEAD_DIM

    @pl.when(pl.program_id(1) == 0)
    def _():
        raw_ref[L:2 * L, :] = jnp.zeros((L, conv_dim), BF16)
        state_ref[...] = jnp.zeros(state_ref.shape, F32)
        ek = lax.broadcasted_iota(jnp.int32, (2 * LANES, inner), 0) & (LANES - 1)
        ec = lax.broadcasted_iota(jnp.int32, (2 * LANES, inner), 1) // SSM_HEAD_DIM
        expand_ref[...] = jnp.where(ek == ec, 1.0, 0.0).astype(BF16)

    raw_ref[0:L, :] = raw_ref[L:2 * L, :]
    raw_ref[L:2 * L, 0:inner] = xs_ref[...]
    raw_ref[L:2 * L, inner:conv_dim] = bc_ref[...]

    taps = SSM_CONV - 1
    tok3 = lax.broadcasted_iota(jnp.int32, (taps * L, 2 * L), 0)
    src3 = lax.broadcasted_iota(jnp.int32, (taps * L, 2 * L), 1)
    shift_all = jnp.where(src3 == (tok3 & (L - 1)) + (L - taps) + tok3 // L, 1.0, 0.0).astype(BF16)

    def conv_silu(cols):
        raw = raw_ref[:, cols]
        shifted = _dot(shift_all, raw)
        acc = cbias_ref[:, cols] + cw_ref[taps:taps + 1, cols] * raw[L:2 * L].astype(F32)
        for k in range(taps):
            acc = acc + cw_ref[k:k + 1, cols] * shifted[k * L:(k + 1) * L]
        return _silu(acc)

    xs = conv_silu(slice(0, inner))
    bcm = conv_silu(slice(inner, conv_dim))

    dt = _softplus(dt_ref[...] + dtb_ref[...])
    a = dt * (-jnp.exp(alog_ref[...]))
    t_idx = lax.broadcasted_iota(jnp.int32, (L, L), 0)
    s_idx = lax.broadcasted_iota(jnp.int32, (L, L), 1)
    causal = t_idx >= s_idx
    tril = jnp.where(causal, 1.0, 0.0).astype(BF16)
    cum3 = _dot(tril, jnp.concatenate(_split_bf16(a, 3), axis=1))
    a_cum = cum3[:, :LANES] + cum3[:, LANES:2 * LANES] + cum3[:, 2 * LANES:]
    a_last = a_cum[L - 1:L, :]
    wgt = dt * jnp.exp(a_last - a_cum)
    chunk_decay = jnp.broadcast_to(jnp.exp(a_last), (16, LANES))
    hi, lo_part = _split_bf16(jnp.concatenate([wgt, jnp.exp(a_cum), chunk_decay], axis=0), 2)
    expanded = _dot(jnp.concatenate([hi, lo_part], axis=1), expand_ref[...])
    xw = (xs * expanded[:L]).astype(BF16)
    decay_in = expanded[L:2 * L]
    cd_exp = expanded[2 * L:2 * L + 1]
    a_cum_t = a_cum.T
    dt_t = dt.T

    head_of_lane = lax.broadcasted_iota(jnp.int32, (1, gw), 1) // SSM_HEAD_DIM

    def per_head_rows(x):
        return jnp.concatenate([jnp.where(head_of_lane == hh, x, 0.0).astype(BF16) for hh in range(hpg)], axis=0)

    for grp in range(SSM_GROUPS):
        b_g = bcm[:, grp * SSM_STATE:(grp + 1) * SSM_STATE]
        c_g = bcm[:, gstate + grp * SSM_STATE:gstate + (grp + 1) * SSM_STATE]
        cols = slice(grp * gw, (grp + 1) * gw)
        cb = _dot_nt(c_g.astype(BF16), b_g.astype(BF16))
        prev = state_ref[grp]
        m_parts = []
        for hh in range(hpg):
            head = grp * hpg + hh
            a_col = jnp.broadcast_to(a_cum[:, head:head + 1], (L, L))
            a_row = jnp.broadcast_to(a_cum_t[head:head + 1, :], (L, L))
            decay = jnp.exp(jnp.where(causal, a_col - a_row, -jnp.inf))
            dt_row = jnp.broadcast_to(dt_t[head:head + 1, :], (L, L))
            m_parts.append((cb * decay * dt_row).astype(BF16))
        y_diag = _dot(jnp.concatenate(m_parts, axis=1), per_head_rows(xs[:, cols]))
        y_ref[:, cols] = y_diag + _dot(c_g.astype(BF16), prev.astype(BF16)) * decay_in[:, cols]
        state_ref[grp] = prev * cd_exp[:, cols] + _dot(b_g.T.astype(BF16), xw[:, cols])

    z = z_ref[...].astype(F32)
    yg = (y_ref[...] + xs * dexp_ref[...]) * _silu(z)
    for grp in range(SSM_GROUPS):
        cols = slice(grp * gw, (grp + 1) * gw)
        o_ref[:, cols] = _rmsnorm(yg[:, cols], ng_ref[:, cols]).astype(o_ref.dtype)


def _ssd(zxbc, dt_raw, conv_w, conv_b, dt_bias, a_log, d_exp, norm_gain, batch, seq, inner):
    gstate = SSM_GROUPS * SSM_STATE
    conv_dim = inner + 2 * gstate
    nc = seq // CHUNK

    def row_block(col):
        return pl.BlockSpec((CHUNK, inner), lambda b, c: (b * nc + c, col))

    return pl.pallas_call(
        functools.partial(_ssd_kernel, inner=inner),
        grid=(batch, nc),
        in_specs=[row_block(0), row_block(1), row_block(2),
                  pl.BlockSpec((CHUNK, LANES), lambda b, c: (b * nc + c, 0)),
                  _resident((SSM_CONV, conv_dim)), _resident((1, conv_dim)),
                  _resident((1, LANES)), _resident((1, LANES)),
                  _resident((1, inner)), _resident((1, inner))],
        out_specs=pl.BlockSpec((CHUNK, inner), lambda b, c: (b * nc + c, 0)),
        out_shape=jax.ShapeDtypeStruct((batch * seq, inner), BF16),
        scratch_shapes=[pltpu.VMEM((2 * CHUNK, conv_dim), BF16),
                        pltpu.VMEM((2 * LANES, inner), BF16),
                        pltpu.VMEM((SSM_GROUPS, SSM_STATE, inner // SSM_GROUPS), F32),
                        pltpu.VMEM((CHUNK, inner), F32)],
        compiler_params=_cparams(("parallel", "arbitrary")),
        name="ssd",
    )(zxbc, zxbc, zxbc, dt_raw, conv_w, conv_b, dt_bias, a_log, d_exp, norm_gain)


def kernel(x, mix_norm, ffn_norm, sb_w_qkv, sb_q_gain, sb_k_gain, sb_w_o, gm_w_in, gm_b_in, gm_v_gain,
           gm_w_s, gm_b_s, gm_w_out, ssm_w_in, ssm_conv_w, ssm_conv_b, ssm_dt_bias, ssm_a_log, ssm_d,
           ssm_norm_gain, ssm_w_out, ffn_w_gu, ffn_w_down):
    batch, seq, d = x.shape
    rows = batch * seq
    depth = mix_norm.shape[0]
    xf = x.reshape(rows, d)
    tail_tm, tail_th = 512, 256
    proj_tm, proj_tn = 512, 512
    gmlp_tm = 512
    for i in range(depth):
        kind, j = i % 3, i // 3
        g_mix = mix_norm[i].reshape(1, d)
        if kind == 0:
            qkv = _norm_matmul(xf, g_mix, sb_w_qkv[j].astype(BF16), proj_tm, proj_tn)
            m = _sb_attention(qkv.reshape(batch, seq, -1), sb_q_gain[j], sb_k_gain[j]).reshape(rows, -1)
            w_proj = sb_w_o[j]
        elif kind == 1:
            half = gm_w_in.shape[2] // 2
            bs_full = jnp.repeat(gm_b_s[j].T, half // GM_GROUPS, axis=1)
            m = _gmlp_front(xf, g_mix, gm_w_in[j].astype(BF16), gm_b_in[j].reshape(1, -1),
                            gm_v_gain[j].reshape(1, -1), gm_w_s[j], bs_full, gmlp_tm)
            w_proj = gm_w_out[j]
        else:
            inner = ssm_w_out.shape[1]
            heads = ssm_dt_bias.shape[1]
            conv_dim = ssm_conv_w.shape[2]
            w_in = ssm_w_in[j]
            w_dt = jnp.pad(w_in[:, inner + conv_dim:], ((0, 0), (0, LANES - heads))).astype(BF16)
            zxbc, dt_raw = _norm_matmul(xf, g_mix, w_in[:, :inner + conv_dim].astype(BF16), proj_tm, proj_tn,
                                        w_f32=w_dt)
            pad_h = (0, LANES - heads)
            m = _ssd(zxbc, dt_raw, ssm_conv_w[j], ssm_conv_b[j].reshape(1, -1),
                     jnp.pad(ssm_dt_bias[j], pad_h).reshape(1, LANES),
                     jnp.pad(ssm_a_log[j], pad_h).reshape(1, LANES),
                     jnp.repeat(ssm_d[j], SSM_HEAD_DIM).reshape(1, inner),
                     ssm_norm_gain[j].reshape(1, inner), batch, seq, inner)
            w_proj = ssm_w_out[j]
        xf = _tail(m, w_proj.astype(BF16), xf, ffn_norm[i].reshape(1, d),
                   ffn_w_gu[i].astype(BF16), ffn_w_down[i].astype(BF16), tail_tm, tail_th)
    return xf.reshape(batch, seq, d)
```

```python
import functools
import math

import jax
import jax.numpy as jnp
from jax import lax
from jax.experimental import pallas as pl
from jax.experimental.pallas import tpu as pltpu

F32 = jnp.float32
BF16 = jnp.bfloat16
EPS = 1e-6
LOG2E = 1.4426950408889634

LANES = 128
V7X_VMEM_BYTES = 64 * 1024 * 1024
VMEM_LIMIT = 56 * 1024 * 1024

HEAD_DIM = 64
CHUNK = 128
SSM_HEAD_DIM = 64
SSM_STATE = 128
SSM_GROUPS = 8
SSM_CONV = 4
GM_GROUPS = 16
SB_DEAD_LOG = -110.0


def _cparams(sem):
    return pltpu.CompilerParams(dimension_semantics=sem, vmem_limit_bytes=VMEM_LIMIT)


def _resident(shape):
    zeros = (0,) * len(shape)
    return pl.BlockSpec(shape, lambda *_: zeros, pipeline_mode=pl.Buffered(1))


def _rmsnorm(x, g):
    ms = jnp.mean(x * x, axis=-1, keepdims=True)
    return x * lax.rsqrt(ms + EPS) * g


def _split_bf16(x, terms):
    parts = []
    r = x
    for _ in range(terms):
        p = r.astype(BF16)
        parts.append(p)
        r = r - p.astype(F32)
    return parts


def _dot(a, b):
    return jnp.dot(a, b, preferred_element_type=F32)


def _dot_nt(a, b):
    return lax.dot_general(a, b, (((1,), (1,)), ((), ())), preferred_element_type=F32)


def _silu(x):
    return x * jax.nn.sigmoid(x)


def _softplus(x):
    return jnp.maximum(x, 0.0) + jnp.log1p(jnp.exp(-jnp.abs(x)))


def _norm_matmul_kernel(x_ref, g_ref, w_ref, *rest, tn):
    h = _rmsnorm(x_ref[...], g_ref[...]).astype(BF16)
    o_ref = rest[-1] if len(rest) == 1 else rest[1]
    for c in range(w_ref.shape[1] // tn):
        o_ref[:, c * tn:(c + 1) * tn] = _dot(h, w_ref[:, c * tn:(c + 1) * tn]).astype(o_ref.dtype)
    if len(rest) == 3:
        rest[2][...] = _dot(h, rest[0][...])


def _norm_matmul(x, g, w, tm, tn, w_f32=None):
    m, k = x.shape
    n = w.shape[1]
    in_specs = [pl.BlockSpec((tm, k), lambda i: (i, 0)), _resident((1, k)), _resident((k, n))]
    out_specs = [pl.BlockSpec((tm, n), lambda i: (i, 0))]
    out_shape = [jax.ShapeDtypeStruct((m, n), BF16)]
    args = [x, g, w]
    if w_f32 is not None:
        n2 = w_f32.shape[1]
        in_specs.append(_resident((k, n2)))
        out_specs.append(pl.BlockSpec((tm, n2), lambda i: (i, 0)))
        out_shape.append(jax.ShapeDtypeStruct((m, n2), F32))
        args.append(w_f32)
    out = pl.pallas_call(
        functools.partial(_norm_matmul_kernel, tn=tn),
        grid=(m // tm,),
        in_specs=in_specs,
        out_specs=out_specs,
        out_shape=out_shape,
        compiler_params=_cparams(("parallel",)),
        name="norm_matmul",
    )(*args)
    return out[0] if w_f32 is None else out


def _tail_kernel(m_ref, wp_ref, x_ref, g_ref, wgu_ref, wd_ref, o_ref, a_ref, *, hidden, th):
    x1 = x_ref[...] + _dot(m_ref[...], wp_ref[...])
    h = _rmsnorm(x1, g_ref[...]).astype(BF16)
    for c in range(hidden // th):
        gate = _dot(h, wgu_ref[:, c * th:(c + 1) * th])
        up = _dot(h, wgu_ref[:, hidden + c * th:hidden + (c + 1) * th])
        a_ref[:, c * th:(c + 1) * th] = (_silu(gate) * up).astype(BF16)
    o_ref[...] = x1 + _dot(a_ref[...], wd_ref[...])


def _tail(m, w_proj, x, g, w_gu, w_down, tm, th):
    rows, d = x.shape
    kin = m.shape[1]
    hidden = w_down.shape[0]
    return pl.pallas_call(
        functools.partial(_tail_kernel, hidden=hidden, th=th),
        grid=(rows // tm,),
        in_specs=[pl.BlockSpec((tm, kin), lambda i: (i, 0)),
                  _resident((kin, d)),
                  pl.BlockSpec((tm, d), lambda i: (i, 0)),
                  _resident((1, d)),
                  _resident((d, 2 * hidden)),
                  _resident((hidden, d))],
        out_specs=pl.BlockSpec((tm, d), lambda i: (i, 0)),
        out_shape=jax.ShapeDtypeStruct((rows, d), F32),
        scratch_shapes=[pltpu.VMEM((tm, hidden), BF16)],
        compiler_params=_cparams(("parallel",)),
        name="tail",
    )(m, w_proj, x, g, w_gu, w_down)


SB_TILE = 64
SB_FAST_BLOCKS = 4
SB_MORE_BLOCKS = 2


def _sb_kernel(q_ref, k_ref, v_ref, qg_ref, kg_ref, o_ref, qn_ref, k2_ref, v2_ref, r_ref, rmax_ref, acc_ref,
               lb_ref, lhs_ref, st_ref, w_ref, *, seq, nt):
    tb = SB_TILE
    fast = SB_FAST_BLOCKS
    prep_rows = 2 * LANES
    lane = lax.broadcasted_iota(jnp.int32, (1, LANES), 1)
    lo = lane < HEAD_DIM

    hk = lax.broadcasted_iota(jnp.int32, (2 * LANES, LANES), 0)
    hn = lax.broadcasted_iota(jnp.int32, (2 * LANES, LANES), 1)
    head_ones = jnp.where(((hk & (LANES - 1)) < HEAD_DIM) == (hn < HEAD_DIM), 1.0, 0.0).astype(BF16)

    q_scale = LOG2E / math.sqrt(HEAD_DIM)
    pad_blocks = fast - 1
    k2_ref[0:pad_blocks * LANES, :] = jnp.zeros((pad_blocks * LANES, LANES), BF16)
    v2_ref[0:pad_blocks * LANES, :] = jnp.zeros((pad_blocks * LANES, LANES), BF16)

    def prep_body(r, carry):
        rows = pl.ds(pl.multiple_of(r * prep_rows, prep_rows), prep_rows)
        kq = jnp.concatenate([k_ref[0, rows, :], q_ref[0, rows, :]], axis=0).astype(F32)
        hi, lo_part = _split_bf16(kq * kq, 2)
        ms = _dot(jnp.concatenate([hi, lo_part], axis=1), head_ones) * (1.0 / HEAD_DIM)
        kq = kq * lax.rsqrt(ms + EPS)
        kn = kq[:prep_rows] * kg_ref[...]
        qn_ref[rows, :] = (kq[prep_rows:] * (qg_ref[...] * q_scale)).astype(BF16)
        vv = v_ref[0, rows, :].astype(F32)
        for part in range(prep_rows // tb):
            sl = slice(part * tb, (part + 1) * tb)
            base = (r * (prep_rows // tb) + part + pad_blocks) * LANES
            head0 = pl.ds(pl.multiple_of(base, LANES), tb)
            head1 = pl.ds(pl.multiple_of(base + tb, tb), tb)
            k2_ref[head0, :] = jnp.where(lo, kn[sl], 0.0).astype(BF16)
            k2_ref[head1, :] = jnp.where(lo, 0.0, kn[sl]).astype(BF16)
            v2_ref[head0, :] = jnp.where(lo, vv[sl], 0.0).astype(BF16)
            v2_ref[head1, :] = jnp.where(lo, 0.0, vv[sl]).astype(BF16)
        return carry

    lax.fori_loop(0, seq // prep_rows, prep_body, 0, unroll=4)

    uj = lax.broadcasted_iota(jnp.int32, (2 * LANES, 2 * LANES), 0) & (LANES - 1)
    uc = lax.broadcasted_iota(jnp.int32, (2 * LANES, 2 * LANES), 1)
    same_head = (uj >= tb) == ((uc & (LANES - 1)) >= tb)
    suffix_total = jnp.where(same_head & ((uc >= LANES) | ((uj & (tb - 1)) > (uc & (tb - 1)))),
                             1.0, 0.0).astype(BF16)
    key_minus_query = ((lax.broadcasted_iota(jnp.int32, (tb, LANES), 1) & (tb - 1))
                       - lax.broadcasted_iota(jnp.int32, (tb, LANES), 0))
    diag_valid = key_minus_query < 0
    dead_log2 = SB_DEAD_LOG * LOG2E

    def log_sigmoids(z):
        log_beta = jnp.minimum(z, 0.0) - jnp.log(1.0 + jnp.exp2(-jnp.abs(z))) * LOG2E
        return log_beta, log_beta - z

    def hi_lo(x):
        hi, lo_part = _split_bf16(x, 2)
        return jnp.concatenate([hi, lo_part], axis=1)

    def set_r(j, r):
        r_ref[j] = r
        rmax_ref[j] = jnp.max(r.reshape(tb // 8, 8, LANES), axis=0)

    def window(g, qts, nblk, swept, first_group=False):
        start = swept == 0 if isinstance(swept, int) else False

        def key_rows(j):
            first = g * nt + j - swept - (nblk - 1) + pad_blocks
            if not start:
                first = jnp.maximum(first, 0)
            return pl.ds(pl.multiple_of(first * LANES, LANES), nblk * LANES)

        def masked(j, i, x):
            if start:
                if first_group and j - i < 0:
                    return jnp.zeros_like(x)
                return jnp.where(diag_valid, x, 0.0) if i == 0 else x
            return jnp.where(g * nt + j - swept - i >= 0, x, 0.0)

        def scores(js):
            for j in js:
                log_beta, log_1m = log_sigmoids(_dot_nt(qts[j], k2_ref[key_rows(j), :]))
                lb_ref[j, :, 0:nblk * LANES] = log_beta
                for i in range(nblk):
                    c0 = (nblk - 1 - i) * LANES
                    row0 = (j * fast + i) * tb
                    lhs_ref[row0:row0 + tb, :] = hi_lo(masked(j, i, log_1m[:, c0:c0 + LANES]))

        def suffix_sums(js):
            for j in js if nblk < fast else js[:1]:
                last = j if nblk < fast else js[-1]
                rows = slice(j * fast * tb, (last * fast + nblk) * tb)
                st_ref[rows, :] = _dot(lhs_ref[rows, :], suffix_total)

        def weights_and_values(js):
            for j in js:
                r_run = None if start else r_ref[j]
                for i in range(nblk):
                    c0 = (nblk - 1 - i) * LANES
                    row0 = (j * fast + i) * tb
                    st = st_ref[row0:row0 + tb, :]
                    arg = lb_ref[j, :, c0:c0 + LANES] + st[:, :LANES]
                    if r_run is not None:
                        arg = arg + r_run
                    w_ref[j, :, c0:c0 + LANES] = masked(j, i, jnp.exp2(arg)).astype(BF16)
                    r_run = st[:, LANES:] if r_run is None else r_run + st[:, LANES:]
                set_r(j, r_run)
                pv = _dot(w_ref[j, :, 0:nblk * LANES], v2_ref[key_rows(j), :])
                acc_ref[j] = pv if start else acc_ref[j] + pv

        half_a, half_b = list(range(nt // 2)), list(range(nt // 2, nt))
        scores(half_a)
        suffix_sums(half_a)
        scores(half_b)
        suffix_sums(half_b)
        weights_and_values(half_a)
        weights_and_values(half_b)

    def group_body(g, carry):
        def tile_rows(j):
            return pl.ds(pl.multiple_of((g * nt + j) * tb, tb), tb)

        qts = [qn_ref[tile_rows(j), :] for j in range(nt)]

        @pl.when(g == 0)
        def _():
            window(0, qts, fast, 0, first_group=True)

        @pl.when(g > 0)
        def _():
            window(g, qts, fast, 0)

        def max_r(swept):
            m = None
            for j in range(nt):
                r_j = jnp.where(g * nt + j >= swept, rmax_ref[j], -jnp.inf)
                m = r_j if m is None else jnp.maximum(m, r_j)
            return jnp.max(m)

        def cond(c):
            _, m = c
            return m > dead_log2

        def body(c):
            swept, _ = c
            window(g, qts, SB_MORE_BLOCKS, swept)
            return swept + SB_MORE_BLOCKS, max_r(swept + SB_MORE_BLOCKS)

        lax.while_loop(cond, body, (jnp.int32(fast), max_r(fast)))
        for j in range(nt):
            o_ref[0, tile_rows(j), :] = acc_ref[j].astype(o_ref.dtype)
        return carry

    lax.fori_loop(0, seq // (tb * nt), group_body, 0)


def _sb_attention(qkv, q_gain, k_gain, nt=32):
    assert nt >= SB_FAST_BLOCKS - 1
    assert 2 * SB_TILE == LANES and HEAD_DIM == SB_TILE
    b, seq, three_d = qkv.shape
    d = three_d // 3
    pairs = d // LANES
    qg = jnp.tile(q_gain.astype(F32), 2).reshape(1, LANES)
    kg = jnp.tile(k_gain.astype(F32), 2).reshape(1, LANES)
    tb, fast = SB_TILE, SB_FAST_BLOCKS
    key_rows = (seq // tb + fast - 1) * LANES
    return pl.pallas_call(
        functools.partial(_sb_kernel, seq=seq, nt=nt),
        grid=(b, pairs),
        in_specs=[pl.BlockSpec((1, seq, LANES), lambda i, p: (i, 0, p)),
                  pl.BlockSpec((1, seq, LANES), lambda i, p: (i, 0, pairs + p)),
                  pl.BlockSpec((1, seq, LANES), lambda i, p: (i, 0, 2 * pairs + p)),
                  pl.BlockSpec((1, LANES), lambda i, p: (0, 0)),
                  pl.BlockSpec((1, LANES), lambda i, p: (0, 0))],
        out_specs=pl.BlockSpec((1, seq, LANES), lambda i, p: (i, 0, p)),
        out_shape=jax.ShapeDtypeStruct((b, seq, d), BF16),
        scratch_shapes=[pltpu.VMEM((seq, LANES), BF16),
                        pltpu.VMEM((key_rows, LANES), BF16),
                        pltpu.VMEM((key_rows, LANES), BF16),
                        pltpu.VMEM((nt, tb, LANES), F32),
                        pltpu.VMEM((nt, 8, LANES), F32),
                        pltpu.VMEM((nt, tb, LANES), F32),
                        pltpu.VMEM((nt, tb, fast * LANES), F32),
                        pltpu.VMEM((nt * fast * tb, 2 * LANES), BF16),
                        pltpu.VMEM((nt * fast * tb, 2 * LANES), F32),
                        pltpu.VMEM((nt, tb, fast * LANES), BF16)],
        compiler_params=_cparams(("parallel", "parallel")),
        name="sb_attention",
    )(qkv, qkv, qkv, qg, kg)


def _gelu(x):
    return 0.5 * x * (1.0 + lax.erf(x * (1.0 / math.sqrt(2.0))))


def _gmlp_kernel(x_ref, g_ref, w_ref, b_ref, vg_ref, ws_ref, bs_ref, o_ref, u_ref, vn_ref, *, half):
    tm = x_ref.shape[0]
    h = _rmsnorm(x_ref[...], g_ref[...]).astype(BF16)
    u_ref[...] = _gelu(_dot(h, w_ref[:, :half]) + b_ref[:, :half])
    v = _gelu(_dot(h, w_ref[:, half:]) + b_ref[:, half:])
    vn_ref[...] = _rmsnorm(v, vg_ref[...]).astype(BF16)
    t_idx = lax.broadcasted_iota(jnp.int32, (CHUNK, CHUNK), 0)
    s_idx = lax.broadcasted_iota(jnp.int32, (CHUNK, CHUNK), 1)
    causal = t_idx >= s_idx
    for grp in range(half // LANES):
        cols = slice(grp * LANES, (grp + 1) * LANES)
        w_s = jnp.where(causal, ws_ref[grp], 0.0).astype(BF16)
        for c in range(tm // CHUNK):
            rows = slice(c * CHUNK, (c + 1) * CHUNK)
            mixed = _dot(w_s, vn_ref[rows, cols]) + bs_ref[:, cols]
            o_ref[rows, cols] = (u_ref[rows, cols] * mixed).astype(o_ref.dtype)


def _gmlp_front(x, g, w_in, b_in, v_gain, w_s, bs_full, tm):
    rows, d = x.shape
    half = w_in.shape[1] // 2
    groups = w_s.shape[0]
    return pl.pallas_call(
        functools.partial(_gmlp_kernel, half=half),
        grid=(rows // tm,),
        in_specs=[pl.BlockSpec((tm, d), lambda i: (i, 0)),
                  _resident((1, d)),
                  _resident((d, 2 * half)),
                  _resident((1, 2 * half)),
                  _resident((1, half)),
                  _resident((groups, CHUNK, CHUNK)),
                  _resident((CHUNK, half))],
        out_specs=pl.BlockSpec((tm, half), lambda i: (i, 0)),
        out_shape=jax.ShapeDtypeStruct((rows, half), BF16),
        scratch_shapes=[pltpu.VMEM((tm, half), F32), pltpu.VMEM((tm, half), BF16)],
        compiler_params=_cparams(("parallel",)),
        name="gmlp_front",
    )(x, g, w_in, b_in, v_gain, w_s, bs_full)


def _ssd_kernel(z_ref, xs_ref, bc_ref, dt_ref, cw_ref, cbias_ref, dtb_ref, alog_ref, dexp_ref, ng_ref, o_ref,
                raw_ref, expand_ref, state_ref, y_ref, *, inner):
    L = CHUNK
    gstate = SSM_GROUPS * SSM_STATE
    conv_dim = inner + 2 * gstate
    gw = inner // SSM_GROUPS
    hpg = gw // SSM_HEAD_DIM

    @pl.when(pl.program_id(1) == 0)
    def _():
        raw_ref[L:2 * L, :] = jnp.zeros((L, conv_dim), BF16)
        state_ref[...] = jnp.zeros(state_ref.shape, F32)
        ek = lax.broadcasted_iota(jnp.int32, (2 * LANES, inner), 0) & (LANES - 1)
        ec = lax.broadcasted_iota(jnp.int32, (2 * LANES, inner), 1) // SSM_HEAD_DIM
        expand_ref[...] = jnp.where(ek == ec, 1.0, 0.0).astype(BF16)

    raw_ref[0:L, :] = raw_ref[L:2 * L, :]
    raw_ref[L:2 * L, 0:inner] = xs_ref[...]
    raw_ref[L:2 * L, inner:conv_dim] = bc_ref[...]

    taps = SSM_CONV - 1
    tok3 = lax.broadcasted_iota(jnp.int32, (taps * L, 2 * L), 0)
    src3 = lax.broadcasted_iota(jnp.int32, (taps * L, 2 * L), 1)
    shift_all = jnp.where(src3 == (tok3 & (L - 1)) + (L - taps) + tok3 // L, 1.0, 0.0).astype(BF16)

    def conv_silu(cols):
        raw = raw_ref[:, cols]
        shifted = _dot(shift_all, raw)
        acc = cbias_ref[:, cols] + cw_ref[taps:taps + 1, cols] * raw[L:2 * L].astype(F32)
        for k in range(taps):
            acc = acc + cw_ref[k:k + 1, cols] * shifted[k * L:(k + 1) * L]
        return _silu(acc)

    xs = conv_silu(slice(0, inner))
    bcm = conv_silu(slice(inner, conv_dim))

    dt = _softplus(dt_ref[...] + dtb_ref[...])
    a = dt * (-jnp.exp(alog_ref[...]))
    t_idx = lax.broadcasted_iota(jnp.int32, (L, L), 0)
    s_idx = lax.broadcasted_iota(jnp.int32, (L, L), 1)
    causal = t_idx >= s_idx
    tril = jnp.where(causal, 1.0, 0.0).astype(BF16)
    cum3 = _dot(tril, jnp.concatenate(_split_bf16(a, 3), axis=1))
    a_cum = cum3[:, :LANES] + cum3[:, LANES:2 * LANES] + cum3[:, 2 * LANES:]
    a_last = a_cum[L - 1:L, :]
    wgt = dt * jnp.exp(a_last - a_cum)
    chunk_decay = jnp.broadcast_to(jnp.exp(a_last), (16, LANES))
    hi, lo_part = _split_bf16(jnp.concatenate([wgt, jnp.exp(a_cum), chunk_decay], axis=0), 2)
    expanded = _dot(jnp.concatenate([hi, lo_part], axis=1), expand_ref[...])
    xw = (xs * expanded[:L]).astype(BF16)
    decay_in = expanded[L:2 * L]
    cd_exp = expanded[2 * L:2 * L + 1]
    a_cum_t = a_cum.T
    dt_t = dt.T

    head_of_lane = lax.broadcasted_iota(jnp.int32, (1, gw), 1) // SSM_HEAD_DIM

    def per_head_rows(x):
        return jnp.concatenate([jnp.where(head_of_lane == hh, x, 0.0).astype(BF16) for hh in range(hpg)], axis=0)

    for grp in range(SSM_GROUPS):
        b_g = bcm[:, grp * SSM_STATE:(grp + 1) * SSM_STATE]
        c_g = bcm[:, gstate + grp * SSM_STATE:gstate + (grp + 1) * SSM_STATE]
        cols = slice(grp * gw, (grp + 1) * gw)
        cb = _dot_nt(c_g.astype(BF16), b_g.astype(BF16))
        prev = state_ref[grp]
        m_parts = []
        for hh in range(hpg):
            head = grp * hpg + hh
            a_col = jnp.broadcast_to(a_cum[:, head:head + 1], (L, L))
            a_row = jnp.broadcast_to(a_cum_t[head:head + 1, :], (L, L))
            decay = jnp.exp(jnp.where(causal, a_col - a_row, -jnp.inf))
            dt_row = jnp.broadcast_to(dt_t[head:head + 1, :], (L, L))
            m_parts.append((cb * decay * dt_row).astype(BF16))
        y_diag = _dot(jnp.concatenate(m_parts, axis=1), per_head_rows(xs[:, cols]))
        y_ref[:, cols] = y_diag + _dot(c_g.astype(BF16), prev.astype(BF16)) * decay_in[:, cols]
        state_ref[grp] = prev * cd_exp[:, cols] + _dot(b_g.T.astype(BF16), xw[:, cols])

    z = z_ref[...].astype(F32)
    yg = (y_ref[...] + xs * dexp_ref[...]) * _silu(z)
    for grp in range(SSM_GROUPS):
        cols = slice(grp * gw, (grp + 1) * gw)
        o_ref[:, cols] = _rmsnorm(yg[:, cols], ng_ref[:, cols]).astype(o_ref.dtype)


def _ssd(zxbc, dt_raw, conv_w, conv_b, dt_bias, a_log, d_exp, norm_gain, batch, seq, inner):
    gstate = SSM_GROUPS * SSM_STATE
    conv_dim = inner + 2 * gstate
    nc = seq // CHUNK

    def row_block(col):
        return pl.BlockSpec((CHUNK, inner), lambda b, c: (b * nc + c, col))

    return pl.pallas_call(
        functools.partial(_ssd_kernel, inner=inner),
        grid=(batch, nc),
        in_specs=[row_block(0), row_block(1), row_block(2),
                  pl.BlockSpec((CHUNK, LANES), lambda b, c: (b * nc + c, 0)),
                  _resident((SSM_CONV, conv_dim)), _resident((1, conv_dim)),
                  _resident((1, LANES)), _resident((1, LANES)),
                  _resident((1, inner)), _resident((1, inner))],
        out_specs=pl.BlockSpec((CHUNK, inner), lambda b, c: (b * nc + c, 0)),
        out_shape=jax.ShapeDtypeStruct((batch * seq, inner), BF16),
        scratch_shapes=[pltpu.VMEM((2 * CHUNK, conv_dim), BF16),
                        pltpu.VMEM((2 * LANES, inner), BF16),
                        pltpu.VMEM((SSM_GROUPS, SSM_STATE, inner // SSM_GROUPS), F32),
                        pltpu.VMEM((CHUNK, inner), F32)],
        compiler_params=_cparams(("parallel", "arbitrary")),
        name="ssd",
    )(zxbc, zxbc, zxbc, dt_raw, conv_w, conv_b, dt_bias, a_log, d_exp, norm_gain)


def kernel(x, mix_norm, ffn_norm, sb_w_qkv, sb_q_gain, sb_k_gain, sb_w_o, gm_w_in, gm_b_in, gm_v_gain,
           gm_w_s, gm_b_s, gm_w_out, ssm_w_in, ssm_conv_w, ssm_conv_b, ssm_dt_bias, ssm_a_log, ssm_d,
           ssm_norm_gain, ssm_w_out, ffn_w_gu, ffn_w_down):
    batch, seq, d = x.shape
    rows = batch * seq
    depth = mix_norm.shape[0]
    xf = x.reshape(rows, d)
    tail_tm, tail_th = 512, 256
    proj_tm, proj_tn = 512, 512
    gmlp_tm = 512
    for i in range(depth):
        kind, j = i % 3, i // 3
        g_mix = mix_norm[i].reshape(1, d)
        if kind == 0:
            qkv = _norm_matmul(xf, g_mix, sb_w_qkv[j].astype(BF16), proj_tm, proj_tn)
            m = _sb_attention(qkv.reshape(batch, seq, -1), sb_q_gain[j], sb_k_gain[j]).reshape(rows, -1)
            w_proj = sb_w_o[j]
        elif kind == 1:
            half = gm_w_in.shape[2] // 2
            bs_full = jnp.repeat(gm_b_s[j].T, half // GM_GROUPS, axis=1)
            m = _gmlp_front(xf, g_mix, gm_w_in[j].astype(BF16), gm_b_in[j].reshape(1, -1),
                            gm_v_gain[j].reshape(1, -1), gm_w_s[j], bs_full, gmlp_tm)
            w_proj = gm_w_out[j]
        else:
            inner = ssm_w_out.shape[1]
            heads = ssm_dt_bias.shape[1]
            conv_dim = ssm_conv_w.shape[2]
            w_in = ssm_w_in[j]
            w_dt = jnp.pad(w_in[:, inner + conv_dim:], ((0, 0), (0, LANES - heads))).astype(BF16)
            zxbc, dt_raw = _norm_matmul(xf, g_mix, w_in[:, :inner + conv_dim].astype(BF16), proj_tm, proj_tn,
                                        w_f32=w_dt)
            pad_h = (0, LANES - heads)
            m = _ssd(zxbc, dt_raw, ssm_conv_w[j], ssm_conv_b[j].reshape(1, -1),
                     jnp.pad(ssm_dt_bias[j], pad_h).reshape(1, LANES),
                     jnp.pad(ssm_a_log[j], pad_h).reshape(1, LANES),
                     jnp.repeat(ssm_d[j], SSM_HEAD_DIM).reshape(1, inner),
                     ssm_norm_gain[j].reshape(1, inner), batch, seq, inner)
            w_proj = ssm_w_out[j]
        xf = _tail(m, w_proj.astype(BF16), xf, ffn_norm[i].reshape(1, d),
                   ffn_w_gu[i].astype(BF16), ffn_w_down[i].astype(BF16), tail_tm, tail_th)
    return xf.reshape(batch, seq, d)
```

```python
import functools
import math

import jax
import jax.numpy as jnp
from jax import lax
from jax.experimental import pallas as pl
from jax.experimental.pallas import tpu as pltpu

F32 = jnp.float32
BF16 = jnp.bfloat16
EPS = 1e-6
LOG2E = 1.4426950408889634

LANES = 128
V7X_VMEM_BYTES = 64 * 1024 * 1024
VMEM_LIMIT = 56 * 1024 * 1024

HEAD_DIM = 64
CHUNK = 128
SSM_HEAD_DIM = 64
SSM_STATE = 128
SSM_GROUPS = 8
SSM_CONV = 4
GM_GROUPS = 16
SB_DEAD_LOG = -110.0


def _cparams(sem):
    return pltpu.CompilerParams(dimension_semantics=sem, vmem_limit_bytes=VMEM_LIMIT)


def _resident(shape):
    zeros = (0,) * len(shape)
    return pl.BlockSpec(shape, lambda *_: zeros, pipeline_mode=pl.Buffered(1))


def _rmsnorm(x, g):
    ms = jnp.mean(x * x, axis=-1, keepdims=True)
    return x * lax.rsqrt(ms + EPS) * g


def _split_bf16(x, terms):
    parts = []
    r = x
    for _ in range(terms):
        p = r.astype(BF16)
        parts.append(p)
        r = r - p.astype(F32)
    return parts


def _dot(a, b):
    return jnp.dot(a, b, preferred_element_type=F32)


def _dot_nt(a, b):
    return lax.dot_general(a, b, (((1,), (1,)), ((), ())), preferred_element_type=F32)


def _silu(x):
    return x * jax.nn.sigmoid(x)


def _softplus(x):
    return jnp.maximum(x, 0.0) + jnp.log1p(jnp.exp(-jnp.abs(x)))


def _norm_matmul_kernel(x_ref, g_ref, w_ref, *rest, tn):
    h = _rmsnorm(x_ref[...], g_ref[...]).astype(BF16)
    o_ref = rest[-1] if len(rest) == 1 else rest[1]
    for c in range(w_ref.shape[1] // tn):
        o_ref[:, c * tn:(c + 1) * tn] = _dot(h, w_ref[:, c * tn:(c + 1) * tn]).astype(o_ref.dtype)
    if len(rest) == 3:
        rest[2][...] = _dot(h, rest[0][...])


def _norm_matmul(x, g, w, tm, tn, w_f32=None):
    m, k = x.shape
    n = w.shape[1]
    in_specs = [pl.BlockSpec((tm, k), lambda i: (i, 0)), _resident((1, k)), _resident((k, n))]
    out_specs = [pl.BlockSpec((tm, n), lambda i: (i, 0))]
    out_shape = [jax.ShapeDtypeStruct((m, n), BF16)]
    args = [x, g, w]
    if w_f32 is not None:
        n2 = w_f32.shape[1]
        in_specs.append(_resident((k, n2)))
        out_specs.append(pl.BlockSpec((tm, n2), lambda i: (i, 0)))
        out_shape.append(jax.ShapeDtypeStruct((m, n2), F32))
        args.append(w_f32)
    out = pl.pallas_call(
        functools.partial(_norm_matmul_kernel, tn=tn),
        grid=(m // tm,),
        in_specs=in_specs,
        out_specs=out_specs,
        out_shape=out_shape,
        compiler_params=_cparams(("parallel",)),
        name="norm_matmul",
    )(*args)
    return out[0] if w_f32 is None else out


def _tail_kernel(m_ref, wp_ref, x_ref, g_ref, wgu_ref, wd_ref, o_ref, a_ref, *, hidden, th):
    x1 = x_ref[...] + _dot(m_ref[...], wp_ref[...])
    h = _rmsnorm(x1, g_ref[...]).astype(BF16)
    for c in range(hidden // th):
        gate = _dot(h, wgu_ref[:, c * th:(c + 1) * th])
        up = _dot(h, wgu_ref[:, hidden + c * th:hidden + (c + 1) * th])
        a_ref[:, c * th:(c + 1) * th] = (_silu(gate) * up).astype(BF16)
    o_ref[...] = x1 + _dot(a_ref[...], wd_ref[...])


def _tail(m, w_proj, x, g, w_gu, w_down, tm, th):
    rows, d = x.shape
    kin = m.shape[1]
    hidden = w_down.shape[0]
    assert hidden % th == 0 and rows % tm == 0
    return pl.pallas_call(
        functools.partial(_tail_kernel, hidden=hidden, th=th),
        grid=(rows // tm,),
        in_specs=[pl.BlockSpec((tm, kin), lambda i: (i, 0)),
                  _resident((kin, d)),
                  pl.BlockSpec((tm, d), lambda i: (i, 0)),
                  _resident((1, d)),
                  _resident((d, 2 * hidden)),
                  _resident((hidden, d))],
        out_specs=pl.BlockSpec((tm, d), lambda i: (i, 0)),
        out_shape=jax.ShapeDtypeStruct((rows, d), F32),
        scratch_shapes=[pltpu.VMEM((tm, hidden), BF16)],
        compiler_params=_cparams(("parallel",)),
        name="tail",
    )(m, w_proj, x, g, w_gu, w_down)


SB_TILE = 64
SB_FAST_BLOCKS = 4
SB_MORE_BLOCKS = 1


def _sb_kernel(q_ref, k_ref, v_ref, qg_ref, kg_ref, o_ref, qn_ref, k2_ref, v2_ref, r_ref, rmax_ref, acc_ref,
               lb_ref, lhs_ref, st_ref, w_ref, *, seq, nt):
    tb = SB_TILE
    fast = SB_FAST_BLOCKS
    prep_rows = 2 * LANES
    lane = lax.broadcasted_iota(jnp.int32, (1, LANES), 1)
    lo = lane < HEAD_DIM

    hk = lax.broadcasted_iota(jnp.int32, (2 * LANES, LANES), 0)
    hn = lax.broadcasted_iota(jnp.int32, (2 * LANES, LANES), 1)
    head_ones = jnp.where(((hk & (LANES - 1)) < HEAD_DIM) == (hn < HEAD_DIM), 1.0, 0.0).astype(BF16)

    q_scale = LOG2E / math.sqrt(HEAD_DIM)
    pad_blocks = fast - 1
    k2_ref[0:pad_blocks * LANES, :] = jnp.zeros((pad_blocks * LANES, LANES), BF16)
    v2_ref[0:pad_blocks * LANES, :] = jnp.zeros((pad_blocks * LANES, LANES), BF16)

    def prep_body(r, carry):
        rows = pl.ds(pl.multiple_of(r * prep_rows, prep_rows), prep_rows)
        kq = jnp.concatenate([k_ref[0, rows, :], q_ref[0, rows, :]], axis=0).astype(F32)
        hi, lo_part = _split_bf16(kq * kq, 2)
        ms = _dot(jnp.concatenate([hi, lo_part], axis=1), head_ones) * (1.0 / HEAD_DIM)
        kq = kq * lax.rsqrt(ms + EPS)
        kn = kq[:prep_rows] * kg_ref[...]
        qn_ref[rows, :] = (kq[prep_rows:] * (qg_ref[...] * q_scale)).astype(BF16)
        vv = v_ref[0, rows, :].astype(F32)
        for part in range(prep_rows // tb):
            sl = slice(part * tb, (part + 1) * tb)
            base = (r * (prep_rows // tb) + part + pad_blocks) * LANES
            head0 = pl.ds(pl.multiple_of(base, LANES), tb)
            head1 = pl.ds(pl.multiple_of(base + tb, tb), tb)
            k2_ref[head0, :] = jnp.where(lo, kn[sl], 0.0).astype(BF16)
            k2_ref[head1, :] = jnp.where(lo, 0.0, kn[sl]).astype(BF16)
            v2_ref[head0, :] = jnp.where(lo, vv[sl], 0.0).astype(BF16)
            v2_ref[head1, :] = jnp.where(lo, 0.0, vv[sl]).astype(BF16)
        return carry

    lax.fori_loop(0, seq // prep_rows, prep_body, 0, unroll=4)

    uj = lax.broadcasted_iota(jnp.int32, (2 * LANES, 2 * LANES), 0) & (LANES - 1)
    uc = lax.broadcasted_iota(jnp.int32, (2 * LANES, 2 * LANES), 1)
    same_head = (uj >= tb) == ((uc & (LANES - 1)) >= tb)
    suffix_total = jnp.where(same_head & ((uc >= LANES) | ((uj & (tb - 1)) > (uc & (tb - 1)))),
                             1.0, 0.0).astype(BF16)
    key_minus_query = ((lax.broadcasted_iota(jnp.int32, (tb, LANES), 1) & (tb - 1))
                       - lax.broadcasted_iota(jnp.int32, (tb, LANES), 0))
    diag_valid = key_minus_query < 0
    dead_log2 = SB_DEAD_LOG * LOG2E

    def log_sigmoids(z):
        log_beta = jnp.minimum(z, 0.0) - jnp.log(1.0 + jnp.exp2(-jnp.abs(z))) * LOG2E
        return log_beta, log_beta - z

    def hi_lo(x):
        hi, lo_part = _split_bf16(x, 2)
        return jnp.concatenate([hi, lo_part], axis=1)

    def set_r(j, r):
        r_ref[j] = r
        rmax_ref[j] = jnp.max(r.reshape(tb // 8, 8, LANES), axis=0)

    def window(g, qts, nblk, swept, first_group=False):
        start = swept == 0 if isinstance(swept, int) else False

        def key_rows(j):
            first = g * nt + j - swept - (nblk - 1) + pad_blocks
            if not start:
                first = jnp.maximum(first, 0)
            return pl.ds(pl.multiple_of(first * LANES, LANES), nblk * LANES)

        def masked(j, i, x):
            if start:
                if first_group and j - i < 0:
                    return jnp.zeros_like(x)
                return jnp.where(diag_valid, x, 0.0) if i == 0 else x
            return jnp.where(g * nt + j - swept - i >= 0, x, 0.0)

        def scores(js):
            for j in js:
                log_beta, log_1m = log_sigmoids(_dot_nt(qts[j], k2_ref[key_rows(j), :]))
                lb_ref[j, :, 0:nblk * LANES] = log_beta
                for i in range(nblk):
                    c0 = (nblk - 1 - i) * LANES
                    row0 = (j * nblk + i) * tb
                    lhs_ref[row0:row0 + tb, :] = hi_lo(masked(j, i, log_1m[:, c0:c0 + LANES]))

        def suffix_sums(js):
            rows = slice(js[0] * nblk * tb, (js[-1] + 1) * nblk * tb)
            st_ref[rows, :] = _dot(lhs_ref[rows, :], suffix_total)

        def weights_and_values(js):
            for j in js:
                r_run = None if start else r_ref[j]
                for i in range(nblk):
                    c0 = (nblk - 1 - i) * LANES
                    row0 = (j * nblk + i) * tb
                    st = st_ref[row0:row0 + tb, :]
                    arg = lb_ref[j, :, c0:c0 + LANES] + st[:, :LANES]
                    if r_run is not None:
                        arg = arg + r_run
                    w_ref[j, :, c0:c0 + LANES] = masked(j, i, jnp.exp2(arg)).astype(BF16)
                    r_run = st[:, LANES:] if r_run is None else r_run + st[:, LANES:]
                set_r(j, r_run)
                pv = _dot(w_ref[j, :, 0:nblk * LANES], v2_ref[key_rows(j), :])
                acc_ref[j] = pv if start else acc_ref[j] + pv

        half_a, half_b = list(range(nt // 2)), list(range(nt // 2, nt))
        scores(half_a)
        suffix_sums(half_a)
        scores(half_b)
        suffix_sums(half_b)
        weights_and_values(half_a)
        weights_and_values(half_b)

    def group_body(g, carry):
        def tile_rows(j):
            return pl.ds(pl.multiple_of((g * nt + j) * tb, tb), tb)

        qts = [qn_ref[tile_rows(j), :] for j in range(nt)]

        @pl.when(g == 0)
        def _():
            window(0, qts, fast, 0, first_group=True)

        @pl.when(g > 0)
        def _():
            window(g, qts, fast, 0)

        def max_r(swept):
            m = None
            for j in range(nt):
                r_j = jnp.where(g * nt + j >= swept, rmax_ref[j], -jnp.inf)
                m = r_j if m is None else jnp.maximum(m, r_j)
            return jnp.max(m)

        def cond(c):
            _, m = c
            return m > dead_log2

        def body(c):
            swept, _ = c
            window(g, qts, SB_MORE_BLOCKS, swept)
            return swept + SB_MORE_BLOCKS, max_r(swept + SB_MORE_BLOCKS)

        lax.while_loop(cond, body, (jnp.int32(fast), max_r(fast)))
        for j in range(nt):
            o_ref[0, tile_rows(j), :] = acc_ref[j].astype(o_ref.dtype)
        return carry

    lax.fori_loop(0, seq // (tb * nt), group_body, 0)


def _sb_attention(qkv, q_gain, k_gain, nt=32):
    assert nt >= SB_FAST_BLOCKS - 1
    assert 2 * SB_TILE == LANES and HEAD_DIM == SB_TILE
    b, seq, three_d = qkv.shape
    d = three_d // 3
    pairs = d // LANES
    qg = jnp.tile(q_gain.astype(F32), 2).reshape(1, LANES)
    kg = jnp.tile(k_gain.astype(F32), 2).reshape(1, LANES)
    tb, fast = SB_TILE, SB_FAST_BLOCKS
    key_rows = (seq // tb + fast - 1) * LANES
    return pl.pallas_call(
        functools.partial(_sb_kernel, seq=seq, nt=nt),
        grid=(b, pairs),
        in_specs=[pl.BlockSpec((1, seq, LANES), lambda i, p: (i, 0, p)),
                  pl.BlockSpec((1, seq, LANES), lambda i, p: (i, 0, pairs + p)),
                  pl.BlockSpec((1, seq, LANES), lambda i, p: (i, 0, 2 * pairs + p)),
                  pl.BlockSpec((1, LANES), lambda i, p: (0, 0)),
                  pl.BlockSpec((1, LANES), lambda i, p: (0, 0))],
        out_specs=pl.BlockSpec((1, seq, LANES), lambda i, p: (i, 0, p)),
        out_shape=jax.ShapeDtypeStruct((b, seq, d), BF16),
        scratch_shapes=[pltpu.VMEM((seq, LANES), BF16),
                        pltpu.VMEM((key_rows, LANES), BF16),
                        pltpu.VMEM((key_rows, LANES), BF16),
                        pltpu.VMEM((nt, tb, LANES), F32),
                        pltpu.VMEM((nt, 8, LANES), F32),
                        pltpu.VMEM((nt, tb, LANES), F32),
                        pltpu.VMEM((nt, tb, fast * LANES), F32),
                        pltpu.VMEM((nt * fast * tb, 2 * LANES), BF16),
                        pltpu.VMEM((nt * fast * tb, 2 * LANES), F32),
                        pltpu.VMEM((nt, tb, fast * LANES), BF16)],
        compiler_params=_cparams(("parallel", "parallel")),
        name="sb_attention",
    )(qkv, qkv, qkv, qg, kg)


def _gelu(x):
    return 0.5 * x * (1.0 + lax.erf(x * (1.0 / math.sqrt(2.0))))


def _gmlp_kernel(x_ref, g_ref, w_ref, b_ref, vg_ref, ws_ref, bs_ref, o_ref, u_ref, vn_ref, *, half):
    tm = x_ref.shape[0]
    h = _rmsnorm(x_ref[...], g_ref[...]).astype(BF16)
    u_ref[...] = _gelu(_dot(h, w_ref[:, :half]) + b_ref[:, :half])
    v = _gelu(_dot(h, w_ref[:, half:]) + b_ref[:, half:])
    vn_ref[...] = _rmsnorm(v, vg_ref[...]).astype(BF16)
    t_idx = lax.broadcasted_iota(jnp.int32, (CHUNK, CHUNK), 0)
    s_idx = lax.broadcasted_iota(jnp.int32, (CHUNK, CHUNK), 1)
    causal = t_idx >= s_idx
    for grp in range(half // LANES):
        cols = slice(grp * LANES, (grp + 1) * LANES)
        w_s = jnp.where(causal, ws_ref[grp], 0.0).astype(BF16)
        for c in range(tm // CHUNK):
            rows = slice(c * CHUNK, (c + 1) * CHUNK)
            mixed = _dot(w_s, vn_ref[rows, cols]) + bs_ref[:, cols]
            o_ref[rows, cols] = (u_ref[rows, cols] * mixed).astype(o_ref.dtype)


def _gmlp_front(x, g, w_in, b_in, v_gain, w_s, bs_full, tm):
    rows, d = x.shape
    half = w_in.shape[1] // 2
    groups = w_s.shape[0]
    return pl.pallas_call(
        functools.partial(_gmlp_kernel, half=half),
        grid=(rows // tm,),
        in_specs=[pl.BlockSpec((tm, d), lambda i: (i, 0)),
                  _resident((1, d)),
                  _resident((d, 2 * half)),
                  _resident((1, 2 * half)),
                  _resident((1, half)),
                  _resident((groups, CHUNK, CHUNK)),
                  _resident((CHUNK, half))],
        out_specs=pl.BlockSpec((tm, half), lambda i: (i, 0)),
        out_shape=jax.ShapeDtypeStruct((rows, half), BF16),
        scratch_shapes=[pltpu.VMEM((tm, half), F32), pltpu.VMEM((tm, half), BF16)],
        compiler_params=_cparams(("parallel",)),
        name="gmlp_front",
    )(x, g, w_in, b_in, v_gain, w_s, bs_full)


def _ssd_kernel(z_ref, xs_ref, bc_ref, dt_ref, cw_ref, cbias_ref, dtb_ref, alog_ref, dexp_ref, ng_ref, o_ref,
                raw_ref, expand_ref, state_ref, y_ref, *, inner, chunks):
    L = CHUNK
    gstate = SSM_GROUPS * SSM_STATE
    conv_dim = inner + 2 * gstate
    gw = inner // SSM_GROUPS
    hpg = gw // SSM_HEAD_DIM

    @pl.when(pl.program_id(1) == 0)
    def _():
        raw_ref[chunks * L:(chunks + 1) * L, :] = jnp.zeros((L, conv_dim), BF16)
        state_ref[...] = jnp.zeros(state_ref.shape, F32)
        ek = lax.broadcasted_iota(jnp.int32, (2 * LANES, inner), 0) & (LANES - 1)
        ec = lax.broadcasted_iota(jnp.int32, (2 * LANES, inner), 1) // SSM_HEAD_DIM
        expand_ref[...] = jnp.where(ek == ec, 1.0, 0.0).astype(BF16)

    raw_ref[0:L, :] = raw_ref[chunks * L:(chunks + 1) * L, :]
    raw_ref[L:(chunks + 1) * L, 0:inner] = xs_ref[...]
    raw_ref[L:(chunks + 1) * L, inner:conv_dim] = bc_ref[...]

    taps = SSM_CONV - 1
    tok3 = lax.broadcasted_iota(jnp.int32, (taps * L, 2 * L), 0)
    src3 = lax.broadcasted_iota(jnp.int32, (taps * L, 2 * L), 1)
    shift_all = jnp.where(src3 == (tok3 & (L - 1)) + (L - taps) + tok3 // L, 1.0, 0.0).astype(BF16)

    t_idx = lax.broadcasted_iota(jnp.int32, (L, L), 0)
    s_idx = lax.broadcasted_iota(jnp.int32, (L, L), 1)
    causal = t_idx >= s_idx
    tril = jnp.where(causal, 1.0, 0.0).astype(BF16)
    head_of_lane = lax.broadcasted_iota(jnp.int32, (1, gw), 1) // SSM_HEAD_DIM

    def per_head_rows(x):
        return jnp.concatenate([jnp.where(head_of_lane == hh, x, 0.0).astype(BF16) for hh in range(hpg)], axis=0)

    def one_chunk(ci):
        rows = slice(ci * L, (ci + 1) * L)

        def conv_silu(cols):
            raw = raw_ref[ci * L:(ci + 2) * L, cols]
            shifted = _dot(shift_all, raw)
            acc = cbias_ref[:, cols] + cw_ref[taps:taps + 1, cols] * raw[L:2 * L].astype(F32)
            for k in range(taps):
                acc = acc + cw_ref[k:k + 1, cols] * shifted[k * L:(k + 1) * L]
            return _silu(acc)

        xs = conv_silu(slice(0, inner))
        bcm = conv_silu(slice(inner, conv_dim))

        dt = _softplus(dt_ref[rows, :] + dtb_ref[...])
        a = dt * (-jnp.exp(alog_ref[...]))
        cum3 = _dot(tril, jnp.concatenate(_split_bf16(a, 3), axis=1))
        a_cum = cum3[:, :LANES] + cum3[:, LANES:2 * LANES] + cum3[:, 2 * LANES:]
        a_last = a_cum[L - 1:L, :]
        wgt = dt * jnp.exp(a_last - a_cum)
        chunk_decay = jnp.broadcast_to(jnp.exp(a_last), (16, LANES))
        hi, lo_part = _split_bf16(jnp.concatenate([wgt, jnp.exp(a_cum), chunk_decay], axis=0), 2)
        expanded = _dot(jnp.concatenate([hi, lo_part], axis=1), expand_ref[...])
        xw = (xs * expanded[:L]).astype(BF16)
        decay_in = expanded[L:2 * L]
        cd_exp = expanded[2 * L:2 * L + 1]
        a_cum_t = a_cum.T
        dt_t = dt.T

        for grp in range(SSM_GROUPS):
            b_g = bcm[:, grp * SSM_STATE:(grp + 1) * SSM_STATE]
            c_g = bcm[:, gstate + grp * SSM_STATE:gstate + (grp + 1) * SSM_STATE]
            cols = slice(grp * gw, (grp + 1) * gw)
            cb = _dot_nt(c_g.astype(BF16), b_g.astype(BF16))
            prev = state_ref[grp]
            m_parts = []
            for hh in range(hpg):
                head = grp * hpg + hh
                a_col = jnp.broadcast_to(a_cum[:, head:head + 1], (L, L))
                a_row = jnp.broadcast_to(a_cum_t[head:head + 1, :], (L, L))
                decay = jnp.exp(jnp.where(causal, a_col - a_row, -jnp.inf))
                dt_row = jnp.broadcast_to(dt_t[head:head + 1, :], (L, L))
                m_parts.append((cb * decay * dt_row).astype(BF16))
            y_diag = _dot(jnp.concatenate(m_parts, axis=1), per_head_rows(xs[:, cols]))
            y_ref[rows, cols] = y_diag + _dot(c_g.astype(BF16), prev.astype(BF16)) * decay_in[:, cols]
            state_ref[grp] = prev * cd_exp[:, cols] + _dot(b_g.T.astype(BF16), xw[:, cols])

        z = z_ref[rows, :].astype(F32)
        yg = (y_ref[rows, :] + xs * dexp_ref[...]) * _silu(z)
        for grp in range(SSM_GROUPS):
            cols = slice(grp * gw, (grp + 1) * gw)
            o_ref[rows, cols] = _rmsnorm(yg[:, cols], ng_ref[:, cols]).astype(o_ref.dtype)

    for ci in range(chunks):
        one_chunk(ci)


def _ssd(zxbc, dt_raw, conv_w, conv_b, dt_bias, a_log, d_exp, norm_gain, batch, seq, inner, chunks):
    gstate = SSM_GROUPS * SSM_STATE
    conv_dim = inner + 2 * gstate
    rows = chunks * CHUNK
    steps = seq // rows

    def row_block(width, col):
        return pl.BlockSpec((rows, width), lambda b, c: (b * steps + c, col))

    return pl.pallas_call(
        functools.partial(_ssd_kernel, inner=inner, chunks=chunks),
        grid=(batch, steps),
        in_specs=[row_block(inner, 0), row_block(inner, 1), row_block(inner, 2), row_block(LANES, 0),
                  _resident((SSM_CONV, conv_dim)), _resident((1, conv_dim)),
                  _resident((1, LANES)), _resident((1, LANES)),
                  _resident((1, inner)), _resident((1, inner))],
        out_specs=row_block(inner, 0),
        out_shape=jax.ShapeDtypeStruct((batch * seq, inner), BF16),
        scratch_shapes=[pltpu.VMEM((rows + CHUNK, conv_dim), BF16),
                        pltpu.VMEM((2 * LANES, inner), BF16),
                        pltpu.VMEM((SSM_GROUPS, SSM_STATE, inner // SSM_GROUPS), F32),
                        pltpu.VMEM((rows, inner), F32)],
        compiler_params=_cparams(("parallel", "arbitrary")),
        name="ssd",
    )(zxbc, zxbc, zxbc, dt_raw, conv_w, conv_b, dt_bias, a_log, d_exp, norm_gain)


def kernel(x, mix_norm, ffn_norm, sb_w_qkv, sb_q_gain, sb_k_gain, sb_w_o, gm_w_in, gm_b_in, gm_v_gain,
           gm_w_s, gm_b_s, gm_w_out, ssm_w_in, ssm_conv_w, ssm_conv_b, ssm_dt_bias, ssm_a_log, ssm_d,
           ssm_norm_gain, ssm_w_out, ffn_w_gu, ffn_w_down):
    batch, seq, d = x.shape
    rows = batch * seq
    depth = mix_norm.shape[0]
    xf = x.reshape(rows, d)
    tail_tm, tail_th = 512, 256
    proj_tm, proj_tn = 512, 512
    gmlp_tm = 1024
    ssd_chunks = 4
    for i in range(depth):
        kind, j = i % 3, i // 3
        g_mix = mix_norm[i].reshape(1, d)
        if kind == 0:
            qkv = _norm_matmul(xf, g_mix, sb_w_qkv[j].astype(BF16), proj_tm, proj_tn)
            m = _sb_attention(qkv.reshape(batch, seq, -1), sb_q_gain[j], sb_k_gain[j]).reshape(rows, -1)
            w_proj = sb_w_o[j]
        elif kind == 1:
            half = gm_w_in.shape[2] // 2
            bs_full = jnp.repeat(gm_b_s[j].T, half // GM_GROUPS, axis=1)
            m = _gmlp_front(xf, g_mix, gm_w_in[j].astype(BF16), gm_b_in[j].reshape(1, -1),
                            gm_v_gain[j].reshape(1, -1), gm_w_s[j], bs_full, gmlp_tm)
            w_proj = gm_w_out[j]
        else:
            inner = ssm_w_out.shape[1]
            heads = ssm_dt_bias.shape[1]
            conv_dim = ssm_conv_w.shape[2]
            w_in = ssm_w_in[j]
            w_dt = jnp.pad(w_in[:, inner + conv_dim:], ((0, 0), (0, LANES - heads))).astype(BF16)
            zxbc, dt_raw = _norm_matmul(xf, g_mix, w_in[:, :inner + conv_dim].astype(BF16), proj_tm, proj_tn,
                                        w_f32=w_dt)
            pad_h = (0, LANES - heads)
            m = _ssd(zxbc, dt_raw, ssm_conv_w[j], ssm_conv_b[j].reshape(1, -1),
                     jnp.pad(ssm_dt_bias[j], pad_h).reshape(1, LANES),
                     jnp.pad(ssm_a_log[j], pad_h).reshape(1, LANES),
                     jnp.repeat(ssm_d[j], SSM_HEAD_DIM).reshape(1, inner),
                     ssm_norm_gain[j].reshape(1, inner), batch, seq, inner, ssd_chunks)
            w_proj = ssm_w_out[j]
        xf = _tail(m, w_proj.astype(BF16), xf, ffn_norm[i].reshape(1, d),
                   ffn_w_gu[i].astype(BF16), ffn_w_down[i].astype(BF16), tail_tm, tail_th)
    return xf.reshape(batch, seq, d)
```

```python
import functools
import math

import jax
import jax.numpy as jnp
from jax import lax
from jax.experimental import pallas as pl
from jax.experimental.pallas import tpu as pltpu

F32 = jnp.float32
BF16 = jnp.bfloat16
EPS = 1e-6
LOG2E = 1.4426950408889634

LANES = 128
V7X_VMEM_BYTES = 64 * 1024 * 1024
VMEM_LIMIT = 56 * 1024 * 1024

HEAD_DIM = 64
CHUNK = 128
SSM_HEAD_DIM = 64
SSM_STATE = 128
SSM_GROUPS = 8
SSM_CONV = 4
GM_GROUPS = 16
SB_DEAD_LOG = -110.0


def _cparams(sem):
    return pltpu.CompilerParams(dimension_semantics=sem, vmem_limit_bytes=VMEM_LIMIT)


def _resident(shape):
    zeros = (0,) * len(shape)
    return pl.BlockSpec(shape, lambda *_: zeros, pipeline_mode=pl.Buffered(1))


def _rmsnorm(x, g):
    ms = jnp.mean(x * x, axis=-1, keepdims=True)
    return x * lax.rsqrt(ms + EPS) * g


def _split_bf16(x, terms):
    parts = []
    r = x
    for _ in range(terms):
        p = r.astype(BF16)
        parts.append(p)
        r = r - p.astype(F32)
    return parts


def _dot(a, b):
    return jnp.dot(a, b, preferred_element_type=F32)


def _dot_nt(a, b):
    return lax.dot_general(a, b, (((1,), (1,)), ((), ())), preferred_element_type=F32)


def _silu(x):
    return x * jax.nn.sigmoid(x)


def _softplus(x):
    return jnp.maximum(x, 0.0) + jnp.log1p(jnp.exp(-jnp.abs(x)))


def _norm_matmul_kernel(x_ref, g_ref, w_ref, *rest, tn):
    h = _rmsnorm(x_ref[...], g_ref[...]).astype(BF16)
    o_ref = rest[-1] if len(rest) == 1 else rest[1]
    for c in range(w_ref.shape[1] // tn):
        o_ref[:, c * tn:(c + 1) * tn] = _dot(h, w_ref[:, c * tn:(c + 1) * tn]).astype(o_ref.dtype)
    if len(rest) == 3:
        rest[2][...] = _dot(h, rest[0][...])


def _norm_matmul(x, g, w, tm, tn, w_f32=None):
    m, k = x.shape
    n = w.shape[1]
    in_specs = [pl.BlockSpec((tm, k), lambda i: (i, 0)), _resident((1, k)), _resident((k, n))]
    out_specs = [pl.BlockSpec((tm, n), lambda i: (i, 0))]
    out_shape = [jax.ShapeDtypeStruct((m, n), BF16)]
    args = [x, g, w]
    if w_f32 is not None:
        n2 = w_f32.shape[1]
        in_specs.append(_resident((k, n2)))
        out_specs.append(pl.BlockSpec((tm, n2), lambda i: (i, 0)))
        out_shape.append(jax.ShapeDtypeStruct((m, n2), F32))
        args.append(w_f32)
    out = pl.pallas_call(
        functools.partial(_norm_matmul_kernel, tn=tn),
        grid=(m // tm,),
        in_specs=in_specs,
        out_specs=out_specs,
        out_shape=out_shape,
        compiler_params=_cparams(("parallel",)),
        name="norm_matmul",
    )(*args)
    return out[0] if w_f32 is None else out


def _tail_kernel(m_ref, wp_ref, x_ref, g_ref, wgu_ref, wd_ref, o_ref, a_ref, *, hidden, th):
    x1 = x_ref[...] + _dot(m_ref[...], wp_ref[...])
    h = _rmsnorm(x1, g_ref[...]).astype(BF16)
    for c in range(hidden // th):
        gate = _dot(h, wgu_ref[:, c * th:(c + 1) * th])
        up = _dot(h, wgu_ref[:, hidden + c * th:hidden + (c + 1) * th])
        a_ref[:, c * th:(c + 1) * th] = (_silu(gate) * up).astype(BF16)
    o_ref[...] = x1 + _dot(a_ref[...], wd_ref[...])


def _tail(m, w_proj, x, g, w_gu, w_down, tm, th):
    rows, d = x.shape
    kin = m.shape[1]
    hidden = w_down.shape[0]
    assert hidden % th == 0 and rows % tm == 0
    return pl.pallas_call(
        functools.partial(_tail_kernel, hidden=hidden, th=th),
        grid=(rows // tm,),
        in_specs=[pl.BlockSpec((tm, kin), lambda i: (i, 0)),
                  _resident((kin, d)),
                  pl.BlockSpec((tm, d), lambda i: (i, 0)),
                  _resident((1, d)),
                  _resident((d, 2 * hidden)),
                  _resident((hidden, d))],
        out_specs=pl.BlockSpec((tm, d), lambda i: (i, 0)),
        out_shape=jax.ShapeDtypeStruct((rows, d), F32),
        scratch_shapes=[pltpu.VMEM((tm, hidden), BF16)],
        compiler_params=_cparams(("parallel",)),
        name="tail",
    )(m, w_proj, x, g, w_gu, w_down)


SB_TILE = 64
SB_FAST_BLOCKS = 4
SB_MORE_BLOCKS = 1


def _sb_kernel(q_ref, k_ref, v_ref, qg_ref, kg_ref, o_ref, qn_ref, k2_ref, v2_ref, r_ref, rmax_ref, acc_ref,
               lb_ref, lhs_ref, st_ref, w_ref, *, seq, nt):
    tb = SB_TILE
    fast = SB_FAST_BLOCKS
    prep_rows = 2 * LANES
    lane = lax.broadcasted_iota(jnp.int32, (1, LANES), 1)
    lo = lane < HEAD_DIM

    hk = lax.broadcasted_iota(jnp.int32, (2 * LANES, LANES), 0)
    hn = lax.broadcasted_iota(jnp.int32, (2 * LANES, LANES), 1)
    head_ones = jnp.where(((hk & (LANES - 1)) < HEAD_DIM) == (hn < HEAD_DIM), 1.0, 0.0).astype(BF16)

    q_scale = LOG2E / math.sqrt(HEAD_DIM)
    pad_blocks = fast - 1
    k2_ref[0:pad_blocks * LANES, :] = jnp.zeros((pad_blocks * LANES, LANES), BF16)
    v2_ref[0:pad_blocks * LANES, :] = jnp.zeros((pad_blocks * LANES, LANES), BF16)

    def prep_body(r, carry):
        rows = pl.ds(pl.multiple_of(r * prep_rows, prep_rows), prep_rows)
        kq = jnp.concatenate([k_ref[0, rows, :], q_ref[0, rows, :]], axis=0).astype(F32)
        hi, lo_part = _split_bf16(kq * kq, 2)
        ms = _dot(jnp.concatenate([hi, lo_part], axis=1), head_ones) * (1.0 / HEAD_DIM)
        kq = kq * lax.rsqrt(ms + EPS)
        kn = kq[:prep_rows] * kg_ref[...]
        qn_ref[rows, :] = (kq[prep_rows:] * (qg_ref[...] * q_scale)).astype(BF16)
        vv = v_ref[0, rows, :].astype(F32)
        for part in range(prep_rows // tb):
            sl = slice(part * tb, (part + 1) * tb)
            base = (r * (prep_rows // tb) + part + pad_blocks) * LANES
            head0 = pl.ds(pl.multiple_of(base, LANES), tb)
            head1 = pl.ds(pl.multiple_of(base + tb, tb), tb)
            k2_ref[head0, :] = jnp.where(lo, kn[sl], 0.0).astype(BF16)
            k2_ref[head1, :] = jnp.where(lo, 0.0, kn[sl]).astype(BF16)
            v2_ref[head0, :] = jnp.where(lo, vv[sl], 0.0).astype(BF16)
            v2_ref[head1, :] = jnp.where(lo, 0.0, vv[sl]).astype(BF16)
        return carry

    lax.fori_loop(0, seq // prep_rows, prep_body, 0, unroll=4)

    uj = lax.broadcasted_iota(jnp.int32, (2 * LANES, 2 * LANES), 0) & (LANES - 1)
    uc = lax.broadcasted_iota(jnp.int32, (2 * LANES, 2 * LANES), 1)
    same_head = (uj >= tb) == ((uc & (LANES - 1)) >= tb)
    suffix_total = jnp.where(same_head & ((uc >= LANES) | ((uj & (tb - 1)) > (uc & (tb - 1)))),
                             1.0, 0.0).astype(BF16)
    key_minus_query = ((lax.broadcasted_iota(jnp.int32, (tb, LANES), 1) & (tb - 1))
                       - lax.broadcasted_iota(jnp.int32, (tb, LANES), 0))
    diag_valid = key_minus_query < 0
    dead_log2 = SB_DEAD_LOG * LOG2E

    def log_sigmoids(z):
        log_beta = jnp.minimum(z, 0.0) - jnp.log(1.0 + jnp.exp2(-jnp.abs(z))) * LOG2E
        return log_beta, log_beta - z

    def hi_lo(x):
        hi, lo_part = _split_bf16(x, 2)
        return jnp.concatenate([hi, lo_part], axis=1)

    def set_r(j, r):
        r_ref[j] = r
        rmax_ref[j] = jnp.max(r.reshape(tb // 8, 8, LANES), axis=0)

    def window(g, qts, nblk, swept, first_group=False):
        start = swept == 0 if isinstance(swept, int) else False

        def key_rows(j):
            first = g * nt + j - swept - (nblk - 1) + pad_blocks
            if not start:
                first = jnp.maximum(first, 0)
            return pl.ds(pl.multiple_of(first * LANES, LANES), nblk * LANES)

        def masked(j, i, x):
            if start:
                if first_group and j - i < 0:
                    return jnp.zeros_like(x)
                return jnp.where(diag_valid, x, 0.0) if i == 0 else x
            return jnp.where(g * nt + j - swept - i >= 0, x, 0.0)

        def scores(js):
            for j in js:
                log_beta, log_1m = log_sigmoids(_dot_nt(qts[j], k2_ref[key_rows(j), :]))
                lb_ref[j, :, 0:nblk * LANES] = log_beta
                for i in range(nblk):
                    c0 = (nblk - 1 - i) * LANES
                    row0 = (j * nblk + i) * tb
                    lhs_ref[row0:row0 + tb, :] = hi_lo(masked(j, i, log_1m[:, c0:c0 + LANES]))

        def suffix_sums(js):
            rows = slice(js[0] * nblk * tb, (js[-1] + 1) * nblk * tb)
            st_ref[rows, :] = _dot(lhs_ref[rows, :], suffix_total)

        def weights_and_values(js):
            for j in js:
                r_run = None if start else r_ref[j]
                for i in range(nblk):
                    c0 = (nblk - 1 - i) * LANES
                    row0 = (j * nblk + i) * tb
                    st = st_ref[row0:row0 + tb, :]
                    arg = lb_ref[j, :, c0:c0 + LANES] + st[:, :LANES]
                    if r_run is not None:
                        arg = arg + r_run
                    w_ref[j, :, c0:c0 + LANES] = masked(j, i, jnp.exp2(arg)).astype(BF16)
                    r_run = st[:, LANES:] if r_run is None else r_run + st[:, LANES:]
                set_r(j, r_run)
                pv = _dot(w_ref[j, :, 0:nblk * LANES], v2_ref[key_rows(j), :])
                acc_ref[j] = pv if start else acc_ref[j] + pv

        half_a, half_b = list(range(nt // 2)), list(range(nt // 2, nt))
        scores(half_a)
        suffix_sums(half_a)
        scores(half_b)
        suffix_sums(half_b)
        weights_and_values(half_a)
        weights_and_values(half_b)

    def group_body(g, carry):
        def tile_rows(j):
            return pl.ds(pl.multiple_of((g * nt + j) * tb, tb), tb)

        qts = [qn_ref[tile_rows(j), :] for j in range(nt)]

        @pl.when(g == 0)
        def _():
            window(0, qts, fast, 0, first_group=True)

        @pl.when(g > 0)
        def _():
            window(g, qts, fast, 0)

        def max_r(swept):
            m = None
            for j in range(nt):
                r_j = jnp.where(g * nt + j >= swept, rmax_ref[j], -jnp.inf)
                m = r_j if m is None else jnp.maximum(m, r_j)
            return jnp.max(m)

        def cond(c):
            _, m = c
            return m > dead_log2

        def body(c):
            swept, _ = c
            window(g, qts, SB_MORE_BLOCKS, swept)
            return swept + SB_MORE_BLOCKS, max_r(swept + SB_MORE_BLOCKS)

        lax.while_loop(cond, body, (jnp.int32(fast), max_r(fast)))
        for j in range(nt):
            o_ref[0, tile_rows(j), :] = acc_ref[j].astype(o_ref.dtype)
        return carry

    lax.fori_loop(0, seq // (tb * nt), group_body, 0)


def _sb_attention(qkv, q_gain, k_gain, nt=32):
    assert nt >= SB_FAST_BLOCKS - 1
    assert 2 * SB_TILE == LANES and HEAD_DIM == SB_TILE
    b, seq, three_d = qkv.shape
    d = three_d // 3
    pairs = d // LANES
    qg = jnp.tile(q_gain.astype(F32), 2).reshape(1, LANES)
    kg = jnp.tile(k_gain.astype(F32), 2).reshape(1, LANES)
    tb, fast = SB_TILE, SB_FAST_BLOCKS
    key_rows = (seq // tb + fast - 1) * LANES
    return pl.pallas_call(
        functools.partial(_sb_kernel, seq=seq, nt=nt),
        grid=(b, pairs),
        in_specs=[pl.BlockSpec((1, seq, LANES), lambda i, p: (i, 0, p)),
                  pl.BlockSpec((1, seq, LANES), lambda i, p: (i, 0, pairs + p)),
                  pl.BlockSpec((1, seq, LANES), lambda i, p: (i, 0, 2 * pairs + p)),
                  pl.BlockSpec((1, LANES), lambda i, p: (0, 0)),
                  pl.BlockSpec((1, LANES), lambda i, p: (0, 0))],
        out_specs=pl.BlockSpec((1, seq, LANES), lambda i, p: (i, 0, p)),
        out_shape=jax.ShapeDtypeStruct((b, seq, d), BF16),
        scratch_shapes=[pltpu.VMEM((seq, LANES), BF16),
                        pltpu.VMEM((key_rows, LANES), BF16),
                        pltpu.VMEM((key_rows, LANES), BF16),
                        pltpu.VMEM((nt, tb, LANES), F32),
                        pltpu.VMEM((nt, 8, LANES), F32),
                        pltpu.VMEM((nt, tb, LANES), F32),
                        pltpu.VMEM((nt, tb, fast * LANES), F32),
                        pltpu.VMEM((nt * fast * tb, 2 * LANES), BF16),
                        pltpu.VMEM((nt * fast * tb, 2 * LANES), F32),
                        pltpu.VMEM((nt, tb, fast * LANES), BF16)],
        compiler_params=_cparams(("parallel", "parallel")),
        name="sb_attention",
    )(qkv, qkv, qkv, qg, kg)


def _gelu(x):
    return 0.5 * x * (1.0 + lax.erf(x * (1.0 / math.sqrt(2.0))))


def _gmlp_kernel(x_ref, g_ref, w_ref, b_ref, vg_ref, ws_ref, bs_ref, o_ref, u_ref, vn_ref, *, half):
    tm = x_ref.shape[0]
    h = _rmsnorm(x_ref[...], g_ref[...]).astype(BF16)
    u_ref[...] = _gelu(_dot(h, w_ref[:, :half]) + b_ref[:, :half])
    v = _gelu(_dot(h, w_ref[:, half:]) + b_ref[:, half:])
    vn_ref[...] = _rmsnorm(v, vg_ref[...]).astype(BF16)
    t_idx = lax.broadcasted_iota(jnp.int32, (CHUNK, CHUNK), 0)
    s_idx = lax.broadcasted_iota(jnp.int32, (CHUNK, CHUNK), 1)
    causal = t_idx >= s_idx
    for grp in range(half // LANES):
        cols = slice(grp * LANES, (grp + 1) * LANES)
        w_s = jnp.where(causal, ws_ref[grp], 0.0).astype(BF16)
        for c in range(tm // CHUNK):
            rows = slice(c * CHUNK, (c + 1) * CHUNK)
            mixed = _dot(w_s, vn_ref[rows, cols]) + bs_ref[:, cols]
            o_ref[rows, cols] = (u_ref[rows, cols] * mixed).astype(o_ref.dtype)


def _gmlp_front(x, g, w_in, b_in, v_gain, w_s, bs_full, tm):
    rows, d = x.shape
    half = w_in.shape[1] // 2
    groups = w_s.shape[0]
    return pl.pallas_call(
        functools.partial(_gmlp_kernel, half=half),
        grid=(rows // tm,),
        in_specs=[pl.BlockSpec((tm, d), lambda i: (i, 0)),
                  _resident((1, d)),
                  _resident((d, 2 * half)),
                  _resident((1, 2 * half)),
                  _resident((1, half)),
                  _resident((groups, CHUNK, CHUNK)),
                  _resident((CHUNK, half))],
        out_specs=pl.BlockSpec((tm, half), lambda i: (i, 0)),
        out_shape=jax.ShapeDtypeStruct((rows, half), BF16),
        scratch_shapes=[pltpu.VMEM((tm, half), F32), pltpu.VMEM((tm, half), BF16)],
        compiler_params=_cparams(("parallel",)),
        name="gmlp_front",
    )(x, g, w_in, b_in, v_gain, w_s, bs_full)


def _ssd_kernel(z_ref, xs_ref, bc_ref, dt_ref, cw_ref, cbias_ref, dtb_ref, alog_ref, dexp_ref, ng_ref, o_ref,
                raw_ref, expand_ref, state_ref, y_ref, *, inner, chunks):
    L = CHUNK
    gstate = SSM_GROUPS * SSM_STATE
    conv_dim = inner + 2 * gstate
    gw = inner // SSM_GROUPS
    hpg = gw // SSM_HEAD_DIM

    @pl.when(pl.program_id(1) == 0)
    def _():
        raw_ref[chunks * L:(chunks + 1) * L, :] = jnp.zeros((L, conv_dim), BF16)
        state_ref[...] = jnp.zeros(state_ref.shape, F32)
        ek = lax.broadcasted_iota(jnp.int32, (2 * LANES, inner), 0) & (LANES - 1)
        ec = lax.broadcasted_iota(jnp.int32, (2 * LANES, inner), 1) // SSM_HEAD_DIM
        expand_ref[...] = jnp.where(ek == ec, 1.0, 0.0).astype(BF16)

    raw_ref[0:L, :] = raw_ref[chunks * L:(chunks + 1) * L, :]
    raw_ref[L:(chunks + 1) * L, 0:inner] = xs_ref[...]
    raw_ref[L:(chunks + 1) * L, inner:conv_dim] = bc_ref[...]

    taps = SSM_CONV - 1
    tok3 = lax.broadcasted_iota(jnp.int32, (taps * L, 2 * L), 0)
    src3 = lax.broadcasted_iota(jnp.int32, (taps * L, 2 * L), 1)
    shift_all = jnp.where(src3 == (tok3 & (L - 1)) + (L - taps) + tok3 // L, 1.0, 0.0).astype(BF16)

    t_idx = lax.broadcasted_iota(jnp.int32, (L, L), 0)
    s_idx = lax.broadcasted_iota(jnp.int32, (L, L), 1)
    causal = t_idx >= s_idx
    tril = jnp.where(causal, 1.0, 0.0).astype(BF16)
    head_of_lane = lax.broadcasted_iota(jnp.int32, (1, gw), 1) // SSM_HEAD_DIM

    def per_head_rows(x):
        return jnp.concatenate([jnp.where(head_of_lane == hh, x, 0.0).astype(BF16) for hh in range(hpg)], axis=0)

    def one_chunk(ci):
        rows = slice(ci * L, (ci + 1) * L)

        def conv_silu(cols):
            raw = raw_ref[ci * L:(ci + 2) * L, cols]
            shifted = _dot(shift_all, raw)
            acc = cbias_ref[:, cols] + cw_ref[taps:taps + 1, cols] * raw[L:2 * L].astype(F32)
            for k in range(taps):
                acc = acc + cw_ref[k:k + 1, cols] * shifted[k * L:(k + 1) * L]
            return _silu(acc)

        xs = conv_silu(slice(0, inner))
        bcm = conv_silu(slice(inner, conv_dim))

        dt = _softplus(dt_ref[rows, :] + dtb_ref[...])
        a = dt * (-jnp.exp(alog_ref[...]))
        cum3 = _dot(tril, jnp.concatenate(_split_bf16(a, 3), axis=1))
        a_cum = cum3[:, :LANES] + cum3[:, LANES:2 * LANES] + cum3[:, 2 * LANES:]
        a_last = a_cum[L - 1:L, :]
        wgt = dt * jnp.exp(a_last - a_cum)
        chunk_decay = jnp.broadcast_to(jnp.exp(a_last), (16, LANES))
        hi, lo_part = _split_bf16(jnp.concatenate([wgt, jnp.exp(a_cum), chunk_decay], axis=0), 2)
        expanded = _dot(jnp.concatenate([hi, lo_part], axis=1), expand_ref[...])
        xw = (xs * expanded[:L]).astype(BF16)
        decay_in = expanded[L:2 * L]
        cd_exp = expanded[2 * L:2 * L + 1]
        a_cum_t = a_cum.T
        dt_t = dt.T

        for grp in range(SSM_GROUPS):
            b_g = bcm[:, grp * SSM_STATE:(grp + 1) * SSM_STATE]
            c_g = bcm[:, gstate + grp * SSM_STATE:gstate + (grp + 1) * SSM_STATE]
            cols = slice(grp * gw, (grp + 1) * gw)
            cb = _dot_nt(c_g.astype(BF16), b_g.astype(BF16))
            prev = state_ref[grp]
            m_parts = []
            for hh in range(hpg):
                head = grp * hpg + hh
                a_col = jnp.broadcast_to(a_cum[:, head:head + 1], (L, L))
                a_row = jnp.broadcast_to(a_cum_t[head:head + 1, :], (L, L))
                decay = jnp.exp(jnp.where(causal, a_col - a_row, -jnp.inf))
                dt_row = jnp.broadcast_to(dt_t[head:head + 1, :], (L, L))
                m_parts.append((cb * decay * dt_row).astype(BF16))
            y_diag = _dot(jnp.concatenate(m_parts, axis=1), per_head_rows(xs[:, cols]))
            y_ref[rows, cols] = y_diag + _dot(c_g.astype(BF16), prev.astype(BF16)) * decay_in[:, cols]
            state_ref[grp] = prev * cd_exp[:, cols] + _dot(b_g.T.astype(BF16), xw[:, cols])

        z = z_ref[rows, :].astype(F32)
        yg = (y_ref[rows, :] + xs * dexp_ref[...]) * _silu(z)
        for grp in range(SSM_GROUPS):
            cols = slice(grp * gw, (grp + 1) * gw)
            o_ref[rows, cols] = _rmsnorm(yg[:, cols], ng_ref[:, cols]).astype(o_ref.dtype)

    for ci in range(chunks):
        one_chunk(ci)


def _ssd(zxbc, dt_raw, conv_w, conv_b, dt_bias, a_log, d_exp, norm_gain, batch, seq, inner, chunks):
    gstate = SSM_GROUPS * SSM_STATE
    conv_dim = inner + 2 * gstate
    rows = chunks * CHUNK
    steps = seq // rows

    def row_block(width, col):
        return pl.BlockSpec((rows, width), lambda b, c: (b * steps + c, col))

    return pl.pallas_call(
        functools.partial(_ssd_kernel, inner=inner, chunks=chunks),
        grid=(batch, steps),
        in_specs=[row_block(inner, 0), row_block(inner, 1), row_block(inner, 2), row_block(LANES, 0),
                  _resident((SSM_CONV, conv_dim)), _resident((1, conv_dim)),
                  _resident((1, LANES)), _resident((1, LANES)),
                  _resident((1, inner)), _resident((1, inner))],
        out_specs=row_block(inner, 0),
        out_shape=jax.ShapeDtypeStruct((batch * seq, inner), BF16),
        scratch_shapes=[pltpu.VMEM((rows + CHUNK, conv_dim), BF16),
                        pltpu.VMEM((2 * LANES, inner), BF16),
                        pltpu.VMEM((SSM_GROUPS, SSM_STATE, inner // SSM_GROUPS), F32),
                        pltpu.VMEM((rows, inner), F32)],
        compiler_params=_cparams(("parallel", "arbitrary")),
        name="ssd",
    )(zxbc, zxbc, zxbc, dt_raw, conv_w, conv_b, dt_bias, a_log, d_exp, norm_gain)


def kernel(x, mix_norm, ffn_norm, sb_w_qkv, sb_q_gain, sb_k_gain, sb_w_o, gm_w_in, gm_b_in, gm_v_gain,
           gm_w_s, gm_b_s, gm_w_out, ssm_w_in, ssm_conv_w, ssm_conv_b, ssm_dt_bias, ssm_a_log, ssm_d,
           ssm_norm_gain, ssm_w_out, ffn_w_gu, ffn_w_down):
    batch, seq, d = x.shape
    rows = batch * seq
    depth = mix_norm.shape[0]
    xf = x.reshape(rows, d)
    tail_tm, tail_th = 1024, 256
    proj_tm, proj_tn = 1024, 512
    gmlp_tm = 1024
    ssd_chunks = 4
    for i in range(depth):
        kind, j = i % 3, i // 3
        g_mix = mix_norm[i].reshape(1, d)
        if kind == 0:
            qkv = _norm_matmul(xf, g_mix, sb_w_qkv[j].astype(BF16), proj_tm, proj_tn)
            m = _sb_attention(qkv.reshape(batch, seq, -1), sb_q_gain[j], sb_k_gain[j]).reshape(rows, -1)
            w_proj = sb_w_o[j]
        elif kind == 1:
            half = gm_w_in.shape[2] // 2
            bs_full = jnp.repeat(gm_b_s[j].T, half // GM_GROUPS, axis=1)
            m = _gmlp_front(xf, g_mix, gm_w_in[j].astype(BF16), gm_b_in[j].reshape(1, -1),
                            gm_v_gain[j].reshape(1, -1), gm_w_s[j], bs_full, gmlp_tm)
            w_proj = gm_w_out[j]
        else:
            inner = ssm_w_out.shape[1]
            heads = ssm_dt_bias.shape[1]
            conv_dim = ssm_conv_w.shape[2]
            w_in = ssm_w_in[j]
            w_dt = jnp.pad(w_in[:, inner + conv_dim:], ((0, 0), (0, LANES - heads))).astype(BF16)
            zxbc, dt_raw = _norm_matmul(xf, g_mix, w_in[:, :inner + conv_dim].astype(BF16), proj_tm, proj_tn,
                                        w_f32=w_dt)
            pad_h = (0, LANES - heads)
            m = _ssd(zxbc, dt_raw, ssm_conv_w[j], ssm_conv_b[j].reshape(1, -1),
                     jnp.pad(ssm_dt_bias[j], pad_h).reshape(1, LANES),
                     jnp.pad(ssm_a_log[j], pad_h).reshape(1, LANES),
                     jnp.repeat(ssm_d[j], SSM_HEAD_DIM).reshape(1, inner),
                     ssm_norm_gain[j].reshape(1, inner), batch, seq, inner, ssd_chunks)
            w_proj = ssm_w_out[j]
        xf = _tail(m, w_proj.astype(BF16), xf, ffn_norm[i].reshape(1, d),
                   ffn_w_gu[i].astype(BF16), ffn_w_down[i].astype(BF16), tail_tm, tail_th)
    return xf.reshape(batch, seq, d)
```

```python
import functools
import math

import jax
import jax.numpy as jnp
from jax import lax
from jax.experimental import pallas as pl
from jax.experimental.pallas import tpu as pltpu

F32 = jnp.float32
BF16 = jnp.bfloat16
EPS = 1e-6
LOG2E = 1.4426950408889634

LANES = 128
V7X_VMEM_BYTES = 64 * 1024 * 1024
VMEM_LIMIT = 56 * 1024 * 1024

HEAD_DIM = 64
CHUNK = 128
SSM_HEAD_DIM = 64
SSM_STATE = 128
SSM_GROUPS = 8
SSM_CONV = 4
GM_GROUPS = 16
SB_DEAD_LOG = -110.0


def _cparams(sem):
    return pltpu.CompilerParams(dimension_semantics=sem, vmem_limit_bytes=VMEM_LIMIT)


def _resident(shape):
    zeros = (0,) * len(shape)
    return pl.BlockSpec(shape, lambda *_: zeros, pipeline_mode=pl.Buffered(1))


def _rmsnorm(x, g):
    ms = jnp.mean(x * x, axis=-1, keepdims=True)
    return x * lax.rsqrt(ms + EPS) * g


def _split_bf16(x, terms):
    parts = []
    r = x
    for _ in range(terms):
        p = r.astype(BF16)
        parts.append(p)
        r = r - p.astype(F32)
    return parts


def _dot(a, b):
    return jnp.dot(a, b, preferred_element_type=F32)


def _dot_nt(a, b):
    return lax.dot_general(a, b, (((1,), (1,)), ((), ())), preferred_element_type=F32)


def _silu(x):
    return x * jax.nn.sigmoid(x)


def _softplus(x):
    return jnp.maximum(x, 0.0) + jnp.log1p(jnp.exp(-jnp.abs(x)))


def _norm_matmul_kernel(x_ref, g_ref, w_ref, *rest, tn):
    h = _rmsnorm(x_ref[...], g_ref[...]).astype(BF16)
    o_ref = rest[-1] if len(rest) == 1 else rest[1]
    for c in range(w_ref.shape[1] // tn):
        o_ref[:, c * tn:(c + 1) * tn] = _dot(h, w_ref[:, c * tn:(c + 1) * tn]).astype(o_ref.dtype)
    if len(rest) == 3:
        rest[2][...] = _dot(h, rest[0][...])


def _qkv_kernel(x_ref, g_ref, w_ref, hg_ref, o_ref, *, tn, norm_cols):
    h = _rmsnorm(x_ref[...], g_ref[...]).astype(BF16)
    lo = lax.broadcasted_iota(jnp.int32, (1, LANES), 1) < HEAD_DIM
    for c in range(w_ref.shape[1] // tn):
        acc = _dot(h, w_ref[:, c * tn:(c + 1) * tn])
        for blk in range(tn // LANES):
            cols = slice(c * tn + blk * LANES, c * tn + (blk + 1) * LANES)
            part = acc[:, blk * LANES:(blk + 1) * LANES]
            if c * tn < norm_cols:
                sq = part * part
                ms = jnp.where(lo, jnp.sum(jnp.where(lo, sq, 0.0), axis=1, keepdims=True),
                               jnp.sum(jnp.where(lo, 0.0, sq), axis=1, keepdims=True)) * (1.0 / HEAD_DIM)
                part = part * lax.rsqrt(ms + EPS) * hg_ref[:, cols]
            o_ref[:, cols] = part.astype(o_ref.dtype)


def _qkv_projection(x, g, w, head_gain, tm, tn, norm_cols):
    m, k = x.shape
    n = w.shape[1]
    assert norm_cols % tn == 0
    return pl.pallas_call(
        functools.partial(_qkv_kernel, tn=tn, norm_cols=norm_cols),
        grid=(m // tm,),
        in_specs=[pl.BlockSpec((tm, k), lambda i: (i, 0)), _resident((1, k)), _resident((k, n)), _resident((1, n))],
        out_specs=pl.BlockSpec((tm, n), lambda i: (i, 0)),
        out_shape=jax.ShapeDtypeStruct((m, n), BF16),
        compiler_params=_cparams(("parallel",)),
        name="qkv_projection",
    )(x, g, w, head_gain)


def _norm_matmul(x, g, w, tm, tn, w_f32=None):
    m, k = x.shape
    n = w.shape[1]
    in_specs = [pl.BlockSpec((tm, k), lambda i: (i, 0)), _resident((1, k)), _resident((k, n))]
    out_specs = [pl.BlockSpec((tm, n), lambda i: (i, 0))]
    out_shape = [jax.ShapeDtypeStruct((m, n), BF16)]
    args = [x, g, w]
    if w_f32 is not None:
        n2 = w_f32.shape[1]
        in_specs.append(_resident((k, n2)))
        out_specs.append(pl.BlockSpec((tm, n2), lambda i: (i, 0)))
        out_shape.append(jax.ShapeDtypeStruct((m, n2), F32))
        args.append(w_f32)
    out = pl.pallas_call(
        functools.partial(_norm_matmul_kernel, tn=tn),
        grid=(m // tm,),
        in_specs=in_specs,
        out_specs=out_specs,
        out_shape=out_shape,
        compiler_params=_cparams(("parallel",)),
        name="norm_matmul",
    )(*args)
    return out[0] if w_f32 is None else out


def _tail_kernel(m_ref, wp_ref, x_ref, g_ref, wgu_ref, wd_ref, o_ref, a_ref, *, hidden, th):
    x1 = x_ref[...] + _dot(m_ref[...], wp_ref[...])
    h = _rmsnorm(x1, g_ref[...]).astype(BF16)
    for c in range(hidden // th):
        gate = _dot(h, wgu_ref[:, c * th:(c + 1) * th])
        up = _dot(h, wgu_ref[:, hidden + c * th:hidden + (c + 1) * th])
        a_ref[:, c * th:(c + 1) * th] = (_silu(gate) * up).astype(BF16)
    o_ref[...] = x1 + _dot(a_ref[...], wd_ref[...])


def _tail(m, w_proj, x, g, w_gu, w_down, tm, th):
    rows, d = x.shape
    kin = m.shape[1]
    hidden = w_down.shape[0]
    assert hidden % th == 0 and rows % tm == 0
    return pl.pallas_call(
        functools.partial(_tail_kernel, hidden=hidden, th=th),
        grid=(rows // tm,),
        in_specs=[pl.BlockSpec((tm, kin), lambda i: (i, 0)),
                  _resident((kin, d)),
                  pl.BlockSpec((tm, d), lambda i: (i, 0)),
                  _resident((1, d)),
                  _resident((d, 2 * hidden)),
                  _resident((hidden, d))],
        out_specs=pl.BlockSpec((tm, d), lambda i: (i, 0)),
        out_shape=jax.ShapeDtypeStruct((rows, d), F32),
        scratch_shapes=[pltpu.VMEM((tm, hidden), BF16)],
        compiler_params=_cparams(("parallel",)),
        name="tail",
    )(m, w_proj, x, g, w_gu, w_down)


SB_TILE = 64
SB_FAST_BLOCKS = 4
SB_MORE_BLOCKS = 1


def _sb_kernel(q_ref, k_ref, v_ref, o_ref, k2_ref, v2_ref, r_ref, rmax_ref, acc_ref,
               lb_ref, lhs_ref, st_ref, w_ref, *, seq, nt):
    tb = SB_TILE
    fast = SB_FAST_BLOCKS
    prep_rows = 2 * LANES
    lane = lax.broadcasted_iota(jnp.int32, (1, LANES), 1)
    lo = lane < HEAD_DIM

    pad_blocks = fast - 1
    k2_ref[0:pad_blocks * LANES, :] = jnp.zeros((pad_blocks * LANES, LANES), BF16)
    v2_ref[0:pad_blocks * LANES, :] = jnp.zeros((pad_blocks * LANES, LANES), BF16)

    def prep_body(r, carry):
        rows = pl.ds(pl.multiple_of(r * prep_rows, prep_rows), prep_rows)
        kk = k_ref[0, rows, :]
        vv = v_ref[0, rows, :]
        zero = jnp.zeros((tb, LANES), BF16)
        for part in range(prep_rows // tb):
            sl = slice(part * tb, (part + 1) * tb)
            base = (r * (prep_rows // tb) + part + pad_blocks) * LANES
            head0 = pl.ds(pl.multiple_of(base, LANES), tb)
            head1 = pl.ds(pl.multiple_of(base + tb, tb), tb)
            k2_ref[head0, :] = jnp.where(lo, kk[sl], zero)
            k2_ref[head1, :] = jnp.where(lo, zero, kk[sl])
            v2_ref[head0, :] = jnp.where(lo, vv[sl], zero)
            v2_ref[head1, :] = jnp.where(lo, zero, vv[sl])
        return carry

    lax.fori_loop(0, seq // prep_rows, prep_body, 0, unroll=4)

    uj = lax.broadcasted_iota(jnp.int32, (2 * LANES, 2 * LANES), 0) & (LANES - 1)
    uc = lax.broadcasted_iota(jnp.int32, (2 * LANES, 2 * LANES), 1)
    same_head = (uj >= tb) == ((uc & (LANES - 1)) >= tb)
    suffix_total = jnp.where(same_head & ((uc >= LANES) | ((uj & (tb - 1)) > (uc & (tb - 1)))),
                             1.0, 0.0).astype(BF16)
    key_minus_query = ((lax.broadcasted_iota(jnp.int32, (tb, LANES), 1) & (tb - 1))
                       - lax.broadcasted_iota(jnp.int32, (tb, LANES), 0))
    diag_valid = key_minus_query < 0
    dead_log2 = SB_DEAD_LOG * LOG2E

    def log_sigmoids(z):
        log_beta = jnp.minimum(z, 0.0) - jnp.log(1.0 + jnp.exp2(-jnp.abs(z))) * LOG2E
        return log_beta, log_beta - z

    def hi_lo(x):
        hi, lo_part = _split_bf16(x, 2)
        return jnp.concatenate([hi, lo_part], axis=1)

    def set_r(j, r):
        r_ref[j] = r
        rmax_ref[j] = jnp.max(r.reshape(tb // 8, 8, LANES), axis=0)

    def window(g, qts, nblk, swept, first_group=False):
        start = swept == 0 if isinstance(swept, int) else False

        def key_rows(j):
            first = g * nt + j - swept - (nblk - 1) + pad_blocks
            if not start:
                first = jnp.maximum(first, 0)
            return pl.ds(pl.multiple_of(first * LANES, LANES), nblk * LANES)

        def masked(j, i, x):
            if start:
                if first_group and j - i < 0:
                    return jnp.zeros_like(x)
                return jnp.where(diag_valid, x, 0.0) if i == 0 else x
            return jnp.where(g * nt + j - swept - i >= 0, x, 0.0)

        def scores(js):
            for j in js:
                log_beta, log_1m = log_sigmoids(_dot_nt(qts[j], k2_ref[key_rows(j), :]))
                lb_ref[j, :, 0:nblk * LANES] = log_beta
                for i in range(nblk):
                    c0 = (nblk - 1 - i) * LANES
                    row0 = (j * nblk + i) * tb
                    lhs_ref[row0:row0 + tb, :] = hi_lo(masked(j, i, log_1m[:, c0:c0 + LANES]))

        def suffix_sums(js):
            rows = slice(js[0] * nblk * tb, (js[-1] + 1) * nblk * tb)
            st_ref[rows, :] = _dot(lhs_ref[rows, :], suffix_total)

        def weights_and_values(js):
            for j in js:
                r_run = None if start else r_ref[j]
                for i in range(nblk):
                    c0 = (nblk - 1 - i) * LANES
                    row0 = (j * nblk + i) * tb
                    st = st_ref[row0:row0 + tb, :]
                    arg = lb_ref[j, :, c0:c0 + LANES] + st[:, :LANES]
                    if r_run is not None:
                        arg = arg + r_run
                    w_ref[j, :, c0:c0 + LANES] = masked(j, i, jnp.exp2(arg)).astype(BF16)
                    r_run = st[:, LANES:] if r_run is None else r_run + st[:, LANES:]
                set_r(j, r_run)
                pv = _dot(w_ref[j, :, 0:nblk * LANES], v2_ref[key_rows(j), :])
                acc_ref[j] = pv if start else acc_ref[j] + pv

        half_a, half_b = list(range(nt // 2)), list(range(nt // 2, nt))
        scores(half_a)
        suffix_sums(half_a)
        scores(half_b)
        suffix_sums(half_b)
        weights_and_values(half_a)
        weights_and_values(half_b)

    def group_body(g, carry):
        def tile_rows(j):
            return pl.ds(pl.multiple_of((g * nt + j) * tb, tb), tb)

        qts = [q_ref[0, tile_rows(j), :] for j in range(nt)]

        @pl.when(g == 0)
        def _():
            window(0, qts, fast, 0, first_group=True)

        @pl.when(g > 0)
        def _():
            window(g, qts, fast, 0)

        def max_r(swept):
            m = None
            for j in range(nt):
                r_j = jnp.where(g * nt + j >= swept, rmax_ref[j], -jnp.inf)
                m = r_j if m is None else jnp.maximum(m, r_j)
            return jnp.max(m)

        def cond(c):
            _, m = c
            return m > dead_log2

        def body(c):
            swept, _ = c
            window(g, qts, SB_MORE_BLOCKS, swept)
            return swept + SB_MORE_BLOCKS, max_r(swept + SB_MORE_BLOCKS)

        lax.while_loop(cond, body, (jnp.int32(fast), max_r(fast)))
        for j in range(nt):
            o_ref[0, tile_rows(j), :] = acc_ref[j].astype(o_ref.dtype)
        return carry

    lax.fori_loop(0, seq // (tb * nt), group_body, 0)


def _sb_attention(qkv, nt=32):
    assert nt >= SB_FAST_BLOCKS - 1
    assert 2 * SB_TILE == LANES and HEAD_DIM == SB_TILE
    b, seq, three_d = qkv.shape
    d = three_d // 3
    pairs = d // LANES
    tb, fast = SB_TILE, SB_FAST_BLOCKS
    key_rows = (seq // tb + fast - 1) * LANES
    return pl.pallas_call(
        functools.partial(_sb_kernel, seq=seq, nt=nt),
        grid=(b, pairs),
        in_specs=[pl.BlockSpec((1, seq, LANES), lambda i, p: (i, 0, p)),
                  pl.BlockSpec((1, seq, LANES), lambda i, p: (i, 0, pairs + p)),
                  pl.BlockSpec((1, seq, LANES), lambda i, p: (i, 0, 2 * pairs + p))],
        out_specs=pl.BlockSpec((1, seq, LANES), lambda i, p: (i, 0, p)),
        out_shape=jax.ShapeDtypeStruct((b, seq, d), BF16),
        scratch_shapes=[pltpu.VMEM((key_rows, LANES), BF16),
                        pltpu.VMEM((key_rows, LANES), BF16),
                        pltpu.VMEM((nt, tb, LANES), F32),
                        pltpu.VMEM((nt, 8, LANES), F32),
                        pltpu.VMEM((nt, tb, LANES), F32),
                        pltpu.VMEM((nt, tb, fast * LANES), F32),
                        pltpu.VMEM((nt * fast * tb, 2 * LANES), BF16),
                        pltpu.VMEM((nt * fast * tb, 2 * LANES), F32),
                        pltpu.VMEM((nt, tb, fast * LANES), BF16)],
        compiler_params=_cparams(("parallel", "parallel")),
        name="sb_attention",
    )(qkv, qkv, qkv)


def _gelu(x):
    return 0.5 * x * (1.0 + lax.erf(x * (1.0 / math.sqrt(2.0))))


def _gmlp_kernel(x_ref, g_ref, w_ref, b_ref, vg_ref, ws_ref, bs_ref, o_ref, u_ref, vn_ref, *, half):
    tm = x_ref.shape[0]
    h = _rmsnorm(x_ref[...], g_ref[...]).astype(BF16)
    u_ref[...] = _gelu(_dot(h, w_ref[:, :half]) + b_ref[:, :half])
    v = _gelu(_dot(h, w_ref[:, half:]) + b_ref[:, half:])
    vn_ref[...] = _rmsnorm(v, vg_ref[...]).astype(BF16)
    t_idx = lax.broadcasted_iota(jnp.int32, (CHUNK, CHUNK), 0)
    s_idx = lax.broadcasted_iota(jnp.int32, (CHUNK, CHUNK), 1)
    causal = t_idx >= s_idx
    for grp in range(half // LANES):
        cols = slice(grp * LANES, (grp + 1) * LANES)
        w_s = jnp.where(causal, ws_ref[grp], 0.0).astype(BF16)
        for c in range(tm // CHUNK):
            rows = slice(c * CHUNK, (c + 1) * CHUNK)
            mixed = _dot(w_s, vn_ref[rows, cols]) + bs_ref[:, cols]
            o_ref[rows, cols] = (u_ref[rows, cols] * mixed).astype(o_ref.dtype)


def _gmlp_front(x, g, w_in, b_in, v_gain, w_s, bs_full, tm):
    rows, d = x.shape
    half = w_in.shape[1] // 2
    groups = w_s.shape[0]
    return pl.pallas_call(
        functools.partial(_gmlp_kernel, half=half),
        grid=(rows // tm,),
        in_specs=[pl.BlockSpec((tm, d), lambda i: (i, 0)),
                  _resident((1, d)),
                  _resident((d, 2 * half)),
                  _resident((1, 2 * half)),
                  _resident((1, half)),
                  _resident((groups, CHUNK, CHUNK)),
                  _resident((CHUNK, half))],
        out_specs=pl.BlockSpec((tm, half), lambda i: (i, 0)),
        out_shape=jax.ShapeDtypeStruct((rows, half), BF16),
        scratch_shapes=[pltpu.VMEM((tm, half), F32), pltpu.VMEM((tm, half), BF16)],
        compiler_params=_cparams(("parallel",)),
        name="gmlp_front",
    )(x, g, w_in, b_in, v_gain, w_s, bs_full)


def _ssd_kernel(z_ref, xs_ref, bc_ref, dt_ref, cw_ref, cbias_ref, dtb_ref, alog_ref, dexp_ref, ng_ref, o_ref,
                raw_ref, expand_ref, state_ref, y_ref, *, inner, chunks):
    L = CHUNK
    gstate = SSM_GROUPS * SSM_STATE
    conv_dim = inner + 2 * gstate
    gw = inner // SSM_GROUPS
    hpg = gw // SSM_HEAD_DIM

    @pl.when(pl.program_id(1) == 0)
    def _():
        raw_ref[chunks * L:(chunks + 1) * L, :] = jnp.zeros((L, conv_dim), BF16)
        state_ref[...] = jnp.zeros(state_ref.shape, F32)
        ek = lax.broadcasted_iota(jnp.int32, (2 * LANES, inner), 0) & (LANES - 1)
        ec = lax.broadcasted_iota(jnp.int32, (2 * LANES, inner), 1) // SSM_HEAD_DIM
        expand_ref[...] = jnp.where(ek == ec, 1.0, 0.0).astype(BF16)

    raw_ref[0:L, :] = raw_ref[chunks * L:(chunks + 1) * L, :]
    raw_ref[L:(chunks + 1) * L, 0:inner] = xs_ref[...]
    raw_ref[L:(chunks + 1) * L, inner:conv_dim] = bc_ref[...]

    taps = SSM_CONV - 1
    tok3 = lax.broadcasted_iota(jnp.int32, (taps * L, 2 * L), 0)
    src3 = lax.broadcasted_iota(jnp.int32, (taps * L, 2 * L), 1)
    shift_all = jnp.where(src3 == (tok3 & (L - 1)) + (L - taps) + tok3 // L, 1.0, 0.0).astype(BF16)

    t_idx = lax.broadcasted_iota(jnp.int32, (L, L), 0)
    s_idx = lax.broadcasted_iota(jnp.int32, (L, L), 1)
    causal = t_idx >= s_idx
    tril = jnp.where(causal, 1.0, 0.0).astype(BF16)
    head_of_lane = lax.broadcasted_iota(jnp.int32, (1, gw), 1) // SSM_HEAD_DIM

    def per_head_rows(x):
        return jnp.concatenate([jnp.where(head_of_lane == hh, x, 0.0).astype(BF16) for hh in range(hpg)], axis=0)

    def one_chunk(ci):
        rows = slice(ci * L, (ci + 1) * L)

        def conv_silu(cols):
            raw = raw_ref[ci * L:(ci + 2) * L, cols]
            shifted = _dot(shift_all, raw)
            acc = cbias_ref[:, cols] + cw_ref[taps:taps + 1, cols] * raw[L:2 * L].astype(F32)
            for k in range(taps):
                acc = acc + cw_ref[k:k + 1, cols] * shifted[k * L:(k + 1) * L]
            return _silu(acc)

        xs = conv_silu(slice(0, inner))
        bcm = conv_silu(slice(inner, conv_dim))

        dt = _softplus(dt_ref[rows, :] + dtb_ref[...])
        a = dt * (-jnp.exp(alog_ref[...]))
        cum3 = _dot(tril, jnp.concatenate(_split_bf16(a, 3), axis=1))
        a_cum = cum3[:, :LANES] + cum3[:, LANES:2 * LANES] + cum3[:, 2 * LANES:]
        a_last = a_cum[L - 1:L, :]
        wgt = dt * jnp.exp(a_last - a_cum)
        chunk_decay = jnp.broadcast_to(jnp.exp(a_last), (16, LANES))
        hi, lo_part = _split_bf16(jnp.concatenate([wgt, jnp.exp(a_cum), chunk_decay], axis=0), 2)
        expanded = _dot(jnp.concatenate([hi, lo_part], axis=1), expand_ref[...])
        xw = (xs * expanded[:L]).astype(BF16)
        decay_in = expanded[L:2 * L]
        cd_exp = expanded[2 * L:2 * L + 1]
        a_cum_t = a_cum.T
        dt_t = dt.T

        for grp in range(SSM_GROUPS):
            b_g = bcm[:, grp * SSM_STATE:(grp + 1) * SSM_STATE]
            c_g = bcm[:, gstate + grp * SSM_STATE:gstate + (grp + 1) * SSM_STATE]
            cols = slice(grp * gw, (grp + 1) * gw)
            cb = _dot_nt(c_g.astype(BF16), b_g.astype(BF16))
            prev = state_ref[grp]
            m_parts = []
            for hh in range(hpg):
                head = grp * hpg + hh
                a_col = jnp.broadcast_to(a_cum[:, head:head + 1], (L, L))
                a_row = jnp.broadcast_to(a_cum_t[head:head + 1, :], (L, L))
                decay = jnp.exp(jnp.where(causal, a_col - a_row, -jnp.inf))
                dt_row = jnp.broadcast_to(dt_t[head:head + 1, :], (L, L))
                m_parts.append((cb * decay * dt_row).astype(BF16))
            y_diag = _dot(jnp.concatenate(m_parts, axis=1), per_head_rows(xs[:, cols]))
            y_ref[rows, cols] = y_diag + _dot(c_g.astype(BF16), prev.astype(BF16)) * decay_in[:, cols]
            state_ref[grp] = prev * cd_exp[:, cols] + _dot(b_g.T.astype(BF16), xw[:, cols])

        z = z_ref[rows, :].astype(F32)
        yg = (y_ref[rows, :] + xs * dexp_ref[...]) * _silu(z)
        for grp in range(SSM_GROUPS):
            cols = slice(grp * gw, (grp + 1) * gw)
            o_ref[rows, cols] = _rmsnorm(yg[:, cols], ng_ref[:, cols]).astype(o_ref.dtype)

    for ci in range(chunks):
        one_chunk(ci)


def _ssd(zxbc, dt_raw, conv_w, conv_b, dt_bias, a_log, d_exp, norm_gain, batch, seq, inner, chunks):
    gstate = SSM_GROUPS * SSM_STATE
    conv_dim = inner + 2 * gstate
    rows = chunks * CHUNK
    steps = seq // rows

    def row_block(width, col):
        return pl.BlockSpec((rows, width), lambda b, c: (b * steps + c, col))

    return pl.pallas_call(
        functools.partial(_ssd_kernel, inner=inner, chunks=chunks),
        grid=(batch, steps),
        in_specs=[row_block(inner, 0), row_block(inner, 1), row_block(inner, 2), row_block(LANES, 0),
                  _resident((SSM_CONV, conv_dim)), _resident((1, conv_dim)),
                  _resident((1, LANES)), _resident((1, LANES)),
                  _resident((1, inner)), _resident((1, inner))],
        out_specs=row_block(inner, 0),
        out_shape=jax.ShapeDtypeStruct((batch * seq, inner), BF16),
        scratch_shapes=[pltpu.VMEM((rows + CHUNK, conv_dim), BF16),
                        pltpu.VMEM((2 * LANES, inner), BF16),
                        pltpu.VMEM((SSM_GROUPS, SSM_STATE, inner // SSM_GROUPS), F32),
                        pltpu.VMEM((rows, inner), F32)],
        compiler_params=_cparams(("parallel", "arbitrary")),
        name="ssd",
    )(zxbc, zxbc, zxbc, dt_raw, conv_w, conv_b, dt_bias, a_log, d_exp, norm_gain)


def kernel(x, mix_norm, ffn_norm, sb_w_qkv, sb_q_gain, sb_k_gain, sb_w_o, gm_w_in, gm_b_in, gm_v_gain,
           gm_w_s, gm_b_s, gm_w_out, ssm_w_in, ssm_conv_w, ssm_conv_b, ssm_dt_bias, ssm_a_log, ssm_d,
           ssm_norm_gain, ssm_w_out, ffn_w_gu, ffn_w_down):
    batch, seq, d = x.shape
    rows = batch * seq
    depth = mix_norm.shape[0]
    xf = x.reshape(rows, d)
    tail_tm, tail_th = 1024, 256
    proj_tm, proj_tn = 1024, 512
    gmlp_tm = 1024
    ssd_chunks = 4
    for i in range(depth):
        kind, j = i % 3, i // 3
        g_mix = mix_norm[i].reshape(1, d)
        if kind == 0:
            heads = sb_w_o.shape[1] // HEAD_DIM
            head_gain = jnp.concatenate([jnp.tile(sb_q_gain[j] * (LOG2E / math.sqrt(HEAD_DIM)), heads),
                                         jnp.tile(sb_k_gain[j], heads),
                                         jnp.ones((heads * HEAD_DIM,), F32)]).reshape(1, -1)
            qkv = _qkv_projection(xf, g_mix, sb_w_qkv[j].astype(BF16), head_gain, proj_tm, proj_tn,
                                  2 * heads * HEAD_DIM)
            m = _sb_attention(qkv.reshape(batch, seq, -1)).reshape(rows, -1)
            w_proj = sb_w_o[j]
        elif kind == 1:
            half = gm_w_in.shape[2] // 2
            bs_full = jnp.repeat(gm_b_s[j].T, half // GM_GROUPS, axis=1)
            m = _gmlp_front(xf, g_mix, gm_w_in[j].astype(BF16), gm_b_in[j].reshape(1, -1),
                            gm_v_gain[j].reshape(1, -1), gm_w_s[j], bs_full, gmlp_tm)
            w_proj = gm_w_out[j]
        else:
            inner = ssm_w_out.shape[1]
            heads = ssm_dt_bias.shape[1]
            conv_dim = ssm_conv_w.shape[2]
            w_in = ssm_w_in[j]
            w_dt = jnp.pad(w_in[:, inner + conv_dim:], ((0, 0), (0, LANES - heads))).astype(BF16)
            zxbc, dt_raw = _norm_matmul(xf, g_mix, w_in[:, :inner + conv_dim].astype(BF16), proj_tm, proj_tn,
                                        w_f32=w_dt)
            pad_h = (0, LANES - heads)
            m = _ssd(zxbc, dt_raw, ssm_conv_w[j], ssm_conv_b[j].reshape(1, -1),
                     jnp.pad(ssm_dt_bias[j], pad_h).reshape(1, LANES),
                     jnp.pad(ssm_a_log[j], pad_h).reshape(1, LANES),
                     jnp.repeat(ssm_d[j], SSM_HEAD_DIM).reshape(1, inner),
                     ssm_norm_gain[j].reshape(1, inner), batch, seq, inner, ssd_chunks)
            w_proj = ssm_w_out[j]
        xf = _tail(m, w_proj.astype(BF16), xf, ffn_norm[i].reshape(1, d),
                   ffn_w_gu[i].astype(BF16), ffn_w_down[i].astype(BF16), tail_tm, tail_th)
    return xf.reshape(batch, seq, d)
```
